```python
import jax, jax.numpy as jnp
from jax import lax
import numpy as np

D_MODEL = 1024
BATCH = 4
SEQ = 4096
DEPTH = 2
DEC_BATCH = 16
DEC_SEQ = 2048
PAST_LEN = 128

GRID_W = 64
D_FF = 2816
N_MOD = 9
LRU_W = 384
LRU_BLOCKS = 6
LRU_BS = LRU_W // LRU_BLOCKS
LRU_C = 8.0
CONV_W = 4
RWKV_HEADS = 4
RWKV_HD = 64
RWKV_W = RWKV_HEADS * RWKV_HD
W_LORA = 64
A_LORA = 64
G_LORA = 128
RWKV_IN = 3 * RWKV_W + W_LORA + A_LORA + G_LORA
ATT_HEADS = 6
ATT_KV = 2
ATT_G = ATT_HEADS // ATT_KV
ATT_HD = 64
ATT_Q = ATT_HEADS * ATT_HD
ATT_KVW = ATT_KV * ATT_HD
Q_BLOCK = 128
ROPE_THETA = 10000.0
ROPE_PAIRS = ATT_HD // 4
D_MIX = LRU_W + RWKV_W + ATT_Q
D_IN = 2 * LRU_W + RWKV_IN + ATT_Q + 2 * ATT_KVW
NORM_EPS = 1e-6
GN_EPS = 64e-5

kernel_name = "hybrid_bidir_hymba_encoder"


def rmsnorm(x, g):
    xf = x.astype(jnp.float32)
    y = xf * lax.rsqrt(jnp.mean(xf * xf, axis=-1, keepdims=True) + NORM_EPS)
    return (y * g.astype(jnp.float32)).astype(x.dtype)


def head_rms(x, g):
    xf = x.astype(jnp.float32)
    return xf * lax.rsqrt(jnp.mean(xf * xf, axis=-1, keepdims=True) + NORM_EPS) * g.astype(jnp.float32)


def swiglu(h, w_in, w_out):
    gate, up = jnp.split(h @ w_in, 2, axis=-1)
    return (jax.nn.silu(gate) * up) @ w_out


def rope_tables(seq):
    n_rows = seq // GRID_W
    row = jnp.repeat(jnp.arange(n_rows, dtype=jnp.float32), GRID_W)
    col = jnp.tile(jnp.arange(GRID_W, dtype=jnp.float32), n_rows)
    inv = ROPE_THETA ** (-jnp.arange(ROPE_PAIRS, dtype=jnp.float32) / ROPE_PAIRS)
    ang = jnp.stack([row[:, None] * inv, col[:, None] * inv], axis=1)
    return jnp.cos(ang), jnp.sin(ang)


def apply_rope2d(x, cos, sin):
    b, s, h, _ = x.shape
    xr = x.reshape(b, s, h, 2, 2, ROPE_PAIRS)
    x1, x2 = xr[..., 0, :], xr[..., 1, :]
    c = cos[None, :, None]
    sn = sin[None, :, None]
    out = jnp.stack([x1 * c - x2 * sn, x2 * c + x1 * sn], axis=-2)
    return out.reshape(b, s, h, ATT_HD)


def lru_combine(e1, e2):
    a1, b1 = e1
    a2, b2 = e2
    return a1 * a2, a2 * b1 + b2


def rglru_mixer(xb, yb, lp):
    b, s, _ = xb.shape
    xf = xb.astype(jnp.float32)
    left = CONV_W // 2
    xp = jnp.pad(xf, ((0, 0), (left, CONV_W - 1 - left), (0, 0)))
    w = lp["lru_conv_w"].astype(jnp.float32)
    xc = sum(xp[:, j:j + s] * w[j] for j in range(CONV_W)) + lp["lru_conv_b"].astype(jnp.float32)
    xblk = xc.reshape(b, s, LRU_BLOCKS, LRU_BS)
    h = jnp.zeros_like(xc)
    for d, rev in ((0, False), (1, True)):
        r = jax.nn.sigmoid(jnp.einsum("bsni,nij->bsnj", xblk, lp["lru_w_gate_a"][d].astype(jnp.float32)).reshape(b, s, LRU_W) + lp["lru_b_gate_a"][d])
        i = jax.nn.sigmoid(jnp.einsum("bsni,nij->bsnj", xblk, lp["lru_w_gate_x"][d].astype(jnp.float32)).reshape(b, s, LRU_W) + lp["lru_b_gate_x"][d])
        log_a = -LRU_C * r * jax.nn.softplus(-lp["lru_lambda"][d].astype(jnp.float32))
        a = jnp.exp(log_a)
        u = jnp.sqrt(-jnp.expm1(2.0 * log_a)) * (i * xc)
        _, hd = lax.associative_scan(lru_combine, (a, u), reverse=rev, axis=1)
        h = h + hd
    return h * jax.nn.gelu(yb.astype(jnp.float32))


def rwkv_scan(r, w, k, v, kk, bb, reverse):
    b, _, h, n = r.shape
    xs = tuple(jnp.moveaxis(t, 1, 0) for t in (r, w, k, v, kk, bb))

    def step(st, inp):
        r_t, w_t, k_t, v_t, kk_t, b_t = inp
        sa = jnp.einsum("bhvk,bhk->bhv", st, -kk_t)
        st = st * w_t[:, :, None, :] + sa[..., None] * b_t[:, :, None, :] + v_t[..., None] * k_t[:, :, None, :]
        y = jnp.einsum("bhvk,bhk->bhv", st, r_t)
        return st, y

    st0 = jnp.zeros((b, h, n, n), jnp.float32)
    _, y = lax.scan(step, st0, xs, reverse=reverse)
    return jnp.moveaxis(y, 0, 1)


def rwkv_mixer(zr, lp):
    b, s, _ = zr.shape
    f = zr.astype(jnp.float32)
    prev = jnp.pad(f[:, :-1], ((0, 0), (1, 0), (0, 0)))
    nxt = jnp.pad(f[:, 1:], ((0, 0), (0, 1), (0, 0)))
    f = f + lp["rwkv_mu"].astype(jnp.float32) * (0.5 * (prev + nxt) - f)
    r, k, v, xw, xa, xg = jnp.split(f, [RWKV_W, 2 * RWKV_W, 3 * RWKV_W, 3 * RWKV_W + W_LORA, 3 * RWKV_W + W_LORA + A_LORA], axis=-1)

    def hd(t):
        return t.reshape(b, s, RWKV_HEADS, RWKV_HD)

    g = jax.nn.sigmoid(xg) @ lp["rwkv_g_up"].astype(jnp.float32)
    kk = hd(k * lp["rwkv_k_k"].astype(jnp.float32))
    kk = kk / jnp.maximum(jnp.sqrt(jnp.sum(kk * kk, axis=-1, keepdims=True)), 1e-12)
    a_lora = xa @ lp["rwkv_a_up"].astype(jnp.float32)
    w_lora = jnp.tanh(xw)
    k_a = lp["rwkv_k_a"].astype(jnp.float32)
    y = jnp.zeros((b, s, RWKV_HEADS, RWKV_HD), jnp.float32)
    for d, rev in ((0, False), (1, True)):
        u = lp["rwkv_w0"][d].astype(jnp.float32) + w_lora @ lp["rwkv_w_up"][d].astype(jnp.float32)
        w = jnp.exp(-jnp.exp(-jax.nn.softplus(-u) - 0.5))
        a = jax.nn.sigmoid(lp["rwkv_a0"][d].astype(jnp.float32) + a_lora)
        kd = k * (1.0 + (a - 1.0) * k_a)
        y = y + rwkv_scan(hd(r), hd(w), hd(kd), hd(v), kk, kk * hd(a), rev)
    mu = jnp.mean(y, axis=-1, keepdims=True)
    var = jnp.mean(jnp.square(y - mu), axis=-1, keepdims=True)
    yn = ((y - mu) * lax.rsqrt(var + GN_EPS)).reshape(b, s, RWKV_W)
    yn = yn * lp["rwkv_ln_g"].astype(jnp.float32) + lp["rwkv_ln_b"].astype(jnp.float32)
    bonus = (jnp.sum(hd(r) * hd(k) * lp["rwkv_r_k"].astype(jnp.float32), axis=-1, keepdims=True) * hd(v)).reshape(b, s, RWKV_W)
    return (yn + bonus) * g


def attention(q, k, v, q_g, k_g, rope):
    b, s, _ = q.shape
    cos, sin = rope
    q = apply_rope2d(head_rms(q.reshape(b, s, ATT_HEADS, ATT_HD), q_g), cos, sin)
    k = apply_rope2d(head_rms(k.reshape(b, s, ATT_KV, ATT_HD), k_g), cos, sin)
    v = v.astype(jnp.float32).reshape(b, s, ATT_KV, ATT_HD)
    nblk = s // Q_BLOCK
    qb = jnp.moveaxis(q.reshape(b, nblk, Q_BLOCK, ATT_KV, ATT_G, ATT_HD), 1, 0)
    scale = ATT_HD ** -0.5

    def block(qi):
        sc = jnp.einsum("bqkgd,bskd->bkgqs", qi, k) * scale
        p = jax.nn.softmax(sc, axis=-1)
        return jnp.einsum("bkgqs,bskd->bqkgd", p, v)

    o = lax.map(block, qb)
    return jnp.moveaxis(o, 0, 1).reshape(b, s, ATT_Q)


def mixer(h, lp, rope):
    z = h @ lp["w_mix_in"]
    o1 = 2 * LRU_W + RWKV_IN
    xb, yb, zr, q, k, v = jnp.split(z, [LRU_W, 2 * LRU_W, o1, o1 + ATT_Q, o1 + ATT_Q + ATT_KVW], axis=-1)
    o_lru = rglru_mixer(xb, yb, lp)
    o_rwkv = rwkv_mixer(zr, lp)
    o_att = attention(q, k, v, lp["attn_q_norm"], lp["attn_k_norm"], rope)
    o = jnp.concatenate([o_lru, o_rwkv, o_att], axis=-1).astype(h.dtype)
    return o @ lp["w_mix_out"]


def layer(x, c, lp, rope):
    mod = (jax.nn.silu(c) @ lp["w_ada"] + lp["b_ada"])[:, None, :]
    sh1, sc1, g1, sh2, sc2, g2, sh3, sc3, g3 = jnp.split(mod, N_MOD, axis=-1)
    h = rmsnorm(x, lp["norm_g"][0]) * (1 + sc1) + sh1
    x = x + 0.5 * g1 * swiglu(h, lp["ffn_w_in"][0], lp["ffn_w_out"][0])
    h = rmsnorm(x, lp["norm_g"][1]) * (1 + sc2) + sh2
    x = x + g2 * mixer(h, lp, rope)
    h = rmsnorm(x, lp["norm_g"][2]) * (1 + sc3) + sh3
    x = x + 0.5 * g3 * swiglu(h, lp["ffn_w_in"][1], lp["ffn_w_out"][1])
    return x


def trunk(x, c, params):
    rope = rope_tables(x.shape[1])
    for l in range(DEPTH):
        lp = {name: arr[l] for name, arr in params.items()}
        x = layer(x, c, lp, rope)
    return x


def setup_inputs(seed: int = 0) -> dict:
    key = jax.random.key(seed)
    ks = jax.random.split(key, 32)
    f32 = jnp.float32

    def nrm(k, shape, s):
        return jax.random.normal(k, shape, f32) * s

    u = jax.random.uniform(ks[13], (DEPTH, 2, LRU_W), f32, minval=0.9, maxval=0.999)
    a_lru = u ** (1.0 / LRU_C)
    return {
        "x_prompt": nrm(ks[0], (BATCH, SEQ, D_MODEL), 1.0),
        "x_sample": nrm(ks[1], (DEC_BATCH, DEC_SEQ, D_MODEL), 1.0),
        "c_prompt": nrm(ks[2], (BATCH, D_MODEL), 1.0),
        "c_sample": nrm(ks[3], (DEC_BATCH, D_MODEL), 1.0),
        "w_ada": nrm(ks[4], (DEPTH, D_MODEL, N_MOD * D_MODEL), 0.5 * D_MODEL ** -0.5),
        "b_ada": nrm(ks[5], (DEPTH, N_MOD * D_MODEL), 0.02),
        "norm_g": 1.0 + nrm(ks[6], (DEPTH, 3, D_MODEL), 0.02),
        "ffn_w_in": nrm(ks[7], (DEPTH, 2, D_MODEL, 2 * D_FF), D_MODEL ** -0.5),
        "ffn_w_out": nrm(ks[8], (DEPTH, 2, D_FF, D_MODEL), D_FF ** -0.5),
        "w_mix_in": nrm(ks[9], (DEPTH, D_MODEL, D_IN), D_MODEL ** -0.5),
        "w_mix_out": nrm(ks[10], (DEPTH, D_MIX, D_MODEL), D_MIX ** -0.5),
        "lru_conv_w": nrm(ks[11], (DEPTH, CONV_W, LRU_W), CONV_W ** -0.5),
        "lru_conv_b": nrm(ks[12], (DEPTH, LRU_W), 0.02),
        "lru_w_gate_a": nrm(ks[14], (DEPTH, 2, LRU_BLOCKS, LRU_BS, LRU_BS), LRU_BS ** -0.5),
        "lru_b_gate_a": nrm(ks[15], (DEPTH, 2, LRU_W), 0.02),
        "lru_w_gate_x": nrm(ks[16], (DEPTH, 2, LRU_BLOCKS, LRU_BS, LRU_BS), LRU_BS ** -0.5),
        "lru_b_gate_x": nrm(ks[17], (DEPTH, 2, LRU_W), 0.02),
        "lru_lambda": jnp.log(a_lru) - jnp.log1p(-a_lru),
        "rwkv_mu": jax.random.uniform(ks[18], (DEPTH, RWKV_IN), f32),
        "rwkv_w_up": nrm(ks[19], (DEPTH, 2, W_LORA, RWKV_W), 0.1),
        "rwkv_w0": jax.random.uniform(ks[20], (DEPTH, 2, RWKV_W), f32, minval=-6.0, maxval=1.0),
        "rwkv_a_up": nrm(ks[21], (DEPTH, A_LORA, RWKV_W), 0.1),
        "rwkv_a0": nrm(ks[22], (DEPTH, 2, RWKV_W), 0.5),
        "rwkv_g_up": nrm(ks[23], (DEPTH, G_LORA, RWKV_W), G_LORA ** -0.5),
        "rwkv_k_k": 0.85 + nrm(ks[24], (DEPTH, RWKV_W), 0.05),
        "rwkv_k_a": 1.0 + nrm(ks[25], (DEPTH, RWKV_W), 0.05),
        "rwkv_r_k": nrm(ks[26], (DEPTH, RWKV_HEADS, RWKV_HD), 0.1),
        "rwkv_ln_g": 1.0 + nrm(ks[27], (DEPTH, RWKV_W), 0.02),
        "rwkv_ln_b": nrm(ks[28], (DEPTH, RWKV_W), 0.02),
        "attn_q_norm": 1.0 + nrm(ks[29], (DEPTH, ATT_HD), 0.02),
        "attn_k_norm": 1.0 + nrm(ks[30], (DEPTH, ATT_HD), 0.02),
    }


def reference(x_prompt, x_sample, c_prompt, c_sample, w_ada, b_ada, norm_g, ffn_w_in, ffn_w_out, w_mix_in, w_mix_out, lru_conv_w, lru_conv_b, lru_w_gate_a, lru_b_gate_a, lru_w_gate_x, lru_b_gate_x, lru_lambda, rwkv_mu, rwkv_w_up, rwkv_w0, rwkv_a_up, rwkv_a0, rwkv_g_up, rwkv_k_k, rwkv_k_a, rwkv_r_k, rwkv_ln_g, rwkv_ln_b, attn_q_norm, attn_k_norm):
    params = {
        "w_ada": w_ada, "b_ada": b_ada, "norm_g": norm_g,
        "ffn_w_in": ffn_w_in, "ffn_w_out": ffn_w_out,
        "w_mix_in": w_mix_in, "w_mix_out": w_mix_out,
        "lru_conv_w": lru_conv_w, "lru_conv_b": lru_conv_b,
        "lru_w_gate_a": lru_w_gate_a, "lru_b_gate_a": lru_b_gate_a,
        "lru_w_gate_x": lru_w_gate_x, "lru_b_gate_x": lru_b_gate_x, "lru_lambda": lru_lambda,
        "rwkv_mu": rwkv_mu, "rwkv_w_up": rwkv_w_up, "rwkv_w0": rwkv_w0,
        "rwkv_a_up": rwkv_a_up, "rwkv_a0": rwkv_a0, "rwkv_g_up": rwkv_g_up,
        "rwkv_k_k": rwkv_k_k, "rwkv_k_a": rwkv_k_a, "rwkv_r_k": rwkv_r_k,
        "rwkv_ln_g": rwkv_ln_g, "rwkv_ln_b": rwkv_ln_b,
        "attn_q_norm": attn_q_norm, "attn_k_norm": attn_k_norm,
    }
    y_prompt = trunk(x_prompt, c_prompt, params)
    y_sample = trunk(x_sample, c_sample, params)
    return (y_prompt, y_sample)
```

```python
import functools
import math

import jax
import jax.numpy as jnp
from jax import lax
from jax.experimental import pallas as pl
from jax.experimental.pallas import tpu as pltpu

F32 = jnp.float32
BF16 = jnp.bfloat16

D_MODEL = 1024
D_FF = 2816
N_MOD = 9
GRID_W = 64
LRU_W = 384
LRU_BLOCKS = 6
LRU_BS = LRU_W // LRU_BLOCKS
LRU_C = 8.0
RWKV_HEADS = 4
RWKV_HD = 64
RWKV_W = RWKV_HEADS * RWKV_HD
W_LORA = 64
A_LORA = 64
G_LORA = 128
RWKV_IN = 3 * RWKV_W + W_LORA + A_LORA + G_LORA
ATT_HEADS = 6
ATT_KV = 2
ATT_G = ATT_HEADS // ATT_KV
ATT_HD = 64
ATT_Q = ATT_HEADS * ATT_HD
ATT_KVW = ATT_KV * ATT_HD
ROPE_THETA = 10000.0
ROPE_PAIRS = ATT_HD // 4
D_MIX = LRU_W + RWKV_W + ATT_Q
D_IN = 2 * LRU_W + RWKV_IN + ATT_Q + 2 * ATT_KVW
NORM_EPS = 1e-6
GN_EPS = 64e-5

LANES_V7X = 128
SUBLANES_V7X = 8
VMEM_LIMIT_BYTES = 56 * 1024 * 1024

FFN_TOKENS = 1024
FFN_COLS = 256
MIX_TOKENS = 512
SEQ_TILE = 256
RWKV_CHUNK = 64
ATT_Q_TILE = 256
ATT_KV_TILE = 1024


def _cparams(sem):
    return pltpu.CompilerParams(dimension_semantics=sem, vmem_limit_bytes=VMEM_LIMIT_BYTES)


def _dot(a, b):
    return jnp.dot(a.astype(BF16), b.astype(BF16), preferred_element_type=F32)


def _dot_nt(a, b):
    return lax.dot_general(a.astype(BF16), b.astype(BF16), (((1,), (1,)), ((), ())),
                           preferred_element_type=F32)


def _dot_tn(a, b):
    return lax.dot_general(a.astype(BF16), b.astype(BF16), (((0,), (0,)), ((), ())),
                           preferred_element_type=F32)


def _dot_split(x, w, terms):
    acc = None
    rem = x
    for _ in range(terms):
        hi = rem.astype(BF16)
        part = jnp.dot(hi, w, preferred_element_type=F32)
        acc = part if acc is None else acc + part
        rem = rem - hi.astype(F32)
    return acc


def _sigmoid(x):
    return 1.0 / (1.0 + jnp.exp(-x))


def _silu(x):
    return x * _sigmoid(x)


def _softplus(x):
    return jnp.maximum(x, 0.0) + jnp.log(1.0 + jnp.exp(-jnp.abs(x)))


def _gelu_tanh(x):
    c = math.sqrt(2.0 / math.pi)
    return 0.5 * x * (1.0 + jnp.tanh(c * (x + 0.044715 * (x * x * x))))


def _rms_mod(x, g, scale, shift):
    ms = jnp.mean(x * x, axis=-1, keepdims=True)
    y = x * lax.rsqrt(ms + NORM_EPS) * g
    return y * (1.0 + scale) + shift


def _ada_kernel(c_ref, w_ref, b_ref, o_ref):
    c = c_ref[...]
    o_ref[0] = _dot(_silu(c), w_ref[0]) + b_ref[0]


def _ada_mod(c_all, w_ada, b_ada):
    depth = w_ada.shape[0]
    rows = c_all.shape[0]
    ncol = N_MOD * D_MODEL
    tn = 1152
    return pl.pallas_call(
        _ada_kernel,
        name="ada_mod",
        out_shape=jax.ShapeDtypeStruct((depth, rows, ncol), F32),
        grid=(depth, ncol // tn),
        in_specs=[
            pl.BlockSpec((rows, D_MODEL), lambda l, j: (0, 0)),
            pl.BlockSpec((1, D_MODEL, tn), lambda l, j: (l, 0, j)),
            pl.BlockSpec((1, 1, tn), lambda l, j: (l, 0, j)),
        ],
        out_specs=pl.BlockSpec((1, rows, tn), lambda l, j: (l, 0, j)),
        compiler_params=_cparams(("parallel", "parallel")),
    )(c_all, w_ada, b_ada.reshape(depth, 1, ncol))


def _ffn_kernel(x_ref, mod_ref, g_ref, wg_ref, wu_ref, wo_ref, o_ref, h_ref, acc_ref, *, sub, nj):
    j = pl.program_id(1)

    @pl.when(j == 0)
    def _():
        h = _rms_mod(x_ref[...], g_ref[...], mod_ref[0, 3 * sub + 1:3 * sub + 2, :],
                     mod_ref[0, 3 * sub:3 * sub + 1, :])
        h_ref[...] = h.astype(BF16)
        acc_ref[...] = jnp.zeros_like(acc_ref)

    h = h_ref[...]
    gate = jnp.dot(h, wg_ref[...], preferred_element_type=F32)
    up = jnp.dot(h, wu_ref[...], preferred_element_type=F32)
    act = (_silu(gate) * up).astype(BF16)
    acc_ref[...] += jnp.dot(act, wo_ref[...], preferred_element_type=F32)

    @pl.when(j == nj - 1)
    def _():
        o_ref[...] = x_ref[...] + 0.5 * mod_ref[0, 3 * sub + 2:3 * sub + 3, :] * acc_ref[...]


def _ffn(x, mod, sub, g, w_in, w_out):
    b, s, _ = x.shape
    n = b * s
    tm = min(FFN_TOKENS, s)
    tf = FFN_COLS
    nj = D_FF // tf
    out = pl.pallas_call(
        functools.partial(_ffn_kernel, sub=sub, nj=nj),
        name="ffn",
        out_shape=jax.ShapeDtypeStruct((n, D_MODEL), F32),
        grid=(n // tm, nj),
        in_specs=[
            pl.BlockSpec((tm, D_MODEL), lambda i, j: (i, 0)),
            pl.BlockSpec((1, N_MOD, D_MODEL), lambda i, j: ((i * tm) // s, 0, 0)),
            pl.BlockSpec((1, D_MODEL), lambda i, j: (0, 0)),
            pl.BlockSpec((D_MODEL, tf), lambda i, j: (0, j)),
            pl.BlockSpec((D_MODEL, tf), lambda i, j: (0, j + nj)),
            pl.BlockSpec((tf, D_MODEL), lambda i, j: (j, 0)),
        ],
        out_specs=pl.BlockSpec((tm, D_MODEL), lambda i, j: (i, 0)),
        scratch_shapes=[pltpu.VMEM((tm, D_MODEL), BF16), pltpu.VMEM((tm, D_MODEL), F32)],
        compiler_params=_cparams(("parallel", "arbitrary")),
    )(x.reshape(n, D_MODEL), mod, g.reshape(1, D_MODEL), w_in, w_in, w_out)
    return out.reshape(b, s, D_MODEL)


def _rope_rotate(x, cos, sin_signed):
    outs = []
    for c in range(x.shape[1] // LANES_V7X):
        sl = slice(c * LANES_V7X, (c + 1) * LANES_V7X)
        xc = x[:, sl]
        lane = lax.broadcasted_iota(jnp.int32, xc.shape, 1)
        first = (lane % 32) < 16
        partner = jnp.where(first, pltpu.roll(xc, LANES_V7X - 16, 1), pltpu.roll(xc, 16, 1))
        outs.append(xc * cos[:, sl] + partner * sin_signed[:, sl])
    return jnp.concatenate(outs, axis=1) if len(outs) > 1 else outs[0]


def _mixin_kernel(x_ref, mod_ref, g_ref, w_ref, onesq_ref, gq_ref, gk_ref, cq_ref, sq_ref, ck_ref, sk_ref,
                  xb_ref, yb_ref, zr_ref, q_ref, k_ref, v_ref):
    h = _rms_mod(x_ref[...], g_ref[...], mod_ref[0, 4:5, :], mod_ref[0, 3:4, :]).astype(BF16)

    def proj(lo, hi):
        return jnp.dot(h, w_ref[:, lo:hi], preferred_element_type=F32)

    o1 = 2 * LRU_W + RWKV_IN
    xb_ref[...] = proj(0, LRU_W)
    yb_ref[...] = proj(LRU_W, 2 * LRU_W)
    zr_ref[...] = proj(2 * LRU_W, o1)
    q = proj(o1, o1 + ATT_Q)
    k = proj(o1 + ATT_Q, o1 + ATT_Q + ATT_KVW)
    v = proj(o1 + ATT_Q + ATT_KVW, D_IN)

    ones = onesq_ref[...]
    ssq_q = _dot_split(q * q, ones, 2)
    qn = q * lax.rsqrt(ssq_q * (1.0 / ATT_HD) + NORM_EPS) * gq_ref[...]
    qr = _rope_rotate(qn, cq_ref[...], sq_ref[...]) * (ATT_HD ** -0.5)
    ssq_k = _dot_split(k * k, ones[:ATT_KVW, :ATT_KVW], 2)
    kn = k * lax.rsqrt(ssq_k * (1.0 / ATT_HD) + NORM_EPS) * gk_ref[...]
    kr = _rope_rotate(kn, ck_ref[...], sk_ref[...])
    for hh in range(ATT_HEADS):
        q_ref[0, hh] = qr[:, hh * ATT_HD:(hh + 1) * ATT_HD].astype(BF16)
    for hh in range(ATT_KV):
        k_ref[0, hh] = kr[:, hh * ATT_HD:(hh + 1) * ATT_HD].astype(BF16)
        v_ref[0, hh] = v[:, hh * ATT_HD:(hh + 1) * ATT_HD].astype(BF16)


def _rope_tables(seq):
    n_rows = seq // GRID_W
    row = jnp.repeat(jnp.arange(n_rows, dtype=F32), GRID_W)
    col = jnp.tile(jnp.arange(GRID_W, dtype=F32), n_rows)
    inv = ROPE_THETA ** (-jnp.arange(ROPE_PAIRS, dtype=F32) / ROPE_PAIRS)
    ang_r = row[:, None] * inv
    ang_c = col[:, None] * inv
    cos_h = jnp.concatenate([jnp.cos(ang_r), jnp.cos(ang_r), jnp.cos(ang_c), jnp.cos(ang_c)], axis=1)
    sin_h = jnp.concatenate([-jnp.sin(ang_r), jnp.sin(ang_r), -jnp.sin(ang_c), jnp.sin(ang_c)], axis=1)
    return (jnp.tile(cos_h, (1, ATT_HEADS)), jnp.tile(sin_h, (1, ATT_HEADS)),
            jnp.tile(cos_h, (1, ATT_KV)), jnp.tile(sin_h, (1, ATT_KV)))


def _block_ones(width, block):
    idx = jnp.arange(width)
    return (idx[:, None] // block == idx[None, :] // block).astype(BF16)


def _mixin(x, mod, g, w_in, gq, gk, rope):
    b, s, _ = x.shape
    n = b * s
    tm = min(MIX_TOKENS, s)
    nt = s // tm
    cq, sq, ck, sk = rope
    tok = lambda w: pl.BlockSpec((tm, w), lambda i: (i, 0))
    pos = lambda w: pl.BlockSpec((tm, w), lambda i: (i % nt, 0))
    full = lambda a: pl.BlockSpec(a.shape, lambda i: (0,) * a.ndim)
    ones = _block_ones(ATT_Q, ATT_HD)
    gq_t = jnp.tile(gq.reshape(1, ATT_HD), (1, ATT_HEADS))
    gk_t = jnp.tile(gk.reshape(1, ATT_HD), (1, ATT_KV))
    g2 = g.reshape(1, D_MODEL)
    head = lambda nh: pl.BlockSpec((1, nh, tm, ATT_HD), lambda i: (i // nt, 0, i % nt, 0))
    outs = pl.pallas_call(
        _mixin_kernel,
        name="mix_in",
        out_shape=[
            jax.ShapeDtypeStruct((n, LRU_W), F32),
            jax.ShapeDtypeStruct((n, LRU_W), F32),
            jax.ShapeDtypeStruct((n, RWKV_IN), F32),
            jax.ShapeDtypeStruct((b, ATT_HEADS, s, ATT_HD), BF16),
            jax.ShapeDtypeStruct((b, ATT_KV, s, ATT_HD), BF16),
            jax.ShapeDtypeStruct((b, ATT_KV, s, ATT_HD), BF16),
        ],
        grid=(n // tm,),
        in_specs=[
            tok(D_MODEL),
            pl.BlockSpec((1, N_MOD, D_MODEL), lambda i: ((i * tm) // s, 0, 0)),
            full(g2), full(w_in), full(ones), full(gq_t), full(gk_t),
            pos(ATT_Q), pos(ATT_Q), pos(ATT_KVW), pos(ATT_KVW),
        ],
        out_specs=[tok(LRU_W), tok(LRU_W), tok(RWKV_IN), head(ATT_HEADS), head(ATT_KV), head(ATT_KV)],
        compiler_params=_cparams(("parallel",)),
    )(x.reshape(n, D_MODEL), mod, g2, w_in, ones, gq_t, gk_t, cq, sq, ck, sk)
    xb, yb, zr, q, k, v = outs
    return (xb.reshape(b, s, LRU_W), yb.reshape(b, s, LRU_W), zr.reshape(b, s, RWKV_IN), q, k, v)


def _attn_kernel(q_ref, k_ref, v_ref, o_ref, m_ref, l_ref, acc_ref, *, tq, nk):
    ki = pl.program_id(2)

    @pl.when(ki == 0)
    def _():
        m_ref[...] = jnp.full_like(m_ref, -jnp.inf)
        l_ref[...] = jnp.zeros_like(l_ref)
        acc_ref[...] = jnp.zeros_like(acc_ref)

    for g in range(ATT_KV):
        qg = q_ref[0, ATT_G * g:ATT_G * (g + 1)].reshape(ATT_G * tq, ATT_HD)
        s = lax.dot_general(qg, k_ref[0, g], (((1,), (1,)), ((), ())), preferred_element_type=F32)
        m_prev = m_ref[g]
        m_new = jnp.maximum(m_prev, jnp.max(s, axis=-1, keepdims=True))
        alpha = jnp.exp(m_prev - m_new)
        p = jnp.exp(s - m_new)
        l_ref[g] = alpha * l_ref[g] + jnp.sum(p, axis=-1, keepdims=True)
        acc_ref[g] = alpha * acc_ref[g] + jnp.dot(p.astype(BF16), v_ref[0, g], preferred_element_type=F32)
        m_ref[g] = m_new

    @pl.when(ki == nk - 1)
    def _():
        for g in range(ATT_KV):
            o = acc_ref[g] / l_ref[g]
            o_ref[0, ATT_G * g:ATT_G * (g + 1)] = o.reshape(ATT_G, tq, ATT_HD).astype(BF16)


def _attention(q, k, v):
    b, _, s, _ = q.shape
    tq = min(ATT_Q_TILE, s)
    tk = min(ATT_KV_TILE, s)
    nk = s // tk
    return pl.pallas_call(
        functools.partial(_attn_kernel, tq=tq, nk=nk),
        name="attention",
        out_shape=jax.ShapeDtypeStruct((b, ATT_HEADS, s, ATT_HD), BF16),
        grid=(b, s // tq, nk),
        in_specs=[
            pl.BlockSpec((1, ATT_HEADS, tq, ATT_HD), lambda bi, qi, ki: (bi, 0, qi, 0)),
            pl.BlockSpec((1, ATT_KV, tk, ATT_HD), lambda bi, qi, ki: (bi, 0, ki, 0)),
            pl.BlockSpec((1, ATT_KV, tk, ATT_HD), lambda bi, qi, ki: (bi, 0, ki, 0)),
        ],
        out_specs=pl.BlockSpec((1, ATT_HEADS, tq, ATT_HD), lambda bi, qi, ki: (bi, 0, qi, 0)),
        scratch_shapes=[
            pltpu.VMEM((ATT_KV, ATT_G * tq, 1), F32),
            pltpu.VMEM((ATT_KV, ATT_G * tq, 1), F32),
            pltpu.VMEM((ATT_KV, ATT_G * tq, ATT_HD), F32),
        ],
        compiler_params=_cparams(("parallel", "parallel", "arbitrary")),
    )(q, k, v)


def _shift_rows(x, shift):
    return pltpu.roll(x, shift % x.shape[0], 0)


def _lru_direction(x_ref, xp_ref, xn_ref, cw_ref, cb_ref, wg_ref, bg_ref, lam_ref, h_ref, carry_ref,
                   tile, n_tiles, rev):
    x = x_ref[0]
    t = x.shape[0]
    row = lax.broadcasted_iota(jnp.int32, x.shape, 0)
    pv = jnp.where(tile > 0, xp_ref[0], 0.0)
    nx = jnp.where(tile < n_tiles - 1, xn_ref[0], 0.0)
    xm1 = jnp.where(row == 0, pv[7:8], _shift_rows(x, 1))
    xm2 = jnp.where(row == 0, pv[6:7], jnp.where(row == 1, pv[7:8], _shift_rows(x, 2)))
    xp1 = jnp.where(row == t - 1, nx[0:1], _shift_rows(x, -1))
    cw = cw_ref[...]
    xc = cw[0:1] * xm2 + cw[1:2] * xm1 + cw[2:3] * x + cw[3:4] * xp1 + cb_ref[...]

    gates = _dot(xc, wg_ref[...]) + bg_ref[...]
    r = _sigmoid(gates[:, :LRU_W])
    gi = _sigmoid(gates[:, LRU_W:])
    log_a = (-LRU_C) * r * _softplus(-lam_ref[...])
    a = jnp.exp(log_a)
    u = jnp.sqrt(1.0 - a * a) * (gi * xc)

    step = 1
    while step < t:
        if not rev:
            valid = row >= step
            a_sh = jnp.where(valid, _shift_rows(a, step), 1.0)
            u_sh = jnp.where(valid, _shift_rows(u, step), 0.0)
        else:
            valid = row < t - step
            a_sh = jnp.where(valid, _shift_rows(a, -step), 1.0)
            u_sh = jnp.where(valid, _shift_rows(u, -step), 0.0)
        u = u + a * u_sh
        a = a * a_sh
        step *= 2
    h = u + a * carry_ref[0:1, :]
    h_ref[0] = h
    edge = h[0:1] if rev else h[t - 1:t]
    carry_ref[...] = jnp.broadcast_to(edge, carry_ref.shape)


def _lru_kernel(xf_ref, xfp_ref, xfn_ref, xr_ref, xrp_ref, xrn_ref, cw_ref, cb_ref, wg_ref, bg_ref, lam_ref,
                hf_ref, hr_ref, cf_ref, cr_ref, *, n_tiles):
    i = pl.program_id(1)

    @pl.when(i == 0)
    def _():
        cf_ref[...] = jnp.zeros_like(cf_ref)
        cr_ref[...] = jnp.zeros_like(cr_ref)

    _lru_direction(xf_ref, xfp_ref, xfn_ref, cw_ref, cb_ref, wg_ref.at[0], bg_ref.at[0], lam_ref.at[0],
                   hf_ref, cf_ref, i, n_tiles, False)
    _lru_direction(xr_ref, xrp_ref, xrn_ref, cw_ref, cb_ref, wg_ref.at[1], bg_ref.at[1], lam_ref.at[1],
                   hr_ref, cr_ref, n_tiles - 1 - i, n_tiles, True)


def _halo_specs(width, t, n_tiles, s, rev):
    r8 = t // SUBLANES_V7X
    last8 = s // SUBLANES_V7X - 1
    tile = (lambda i: n_tiles - 1 - i) if rev else (lambda i: i)
    return [
        pl.BlockSpec((1, t, width), lambda b, i: (b, tile(i), 0)),
        pl.BlockSpec((1, SUBLANES_V7X, width), lambda b, i: (b, jnp.maximum(tile(i) * r8 - 1, 0), 0)),
        pl.BlockSpec((1, SUBLANES_V7X, width), lambda b, i: (b, jnp.minimum((tile(i) + 1) * r8, last8), 0)),
    ]


def _lru(xb, conv_w, conv_b, wg, bg, lam):
    b, s, _ = xb.shape
    t = min(SEQ_TILE, s)
    n_tiles = s // t
    full = lambda a: pl.BlockSpec(a.shape, lambda bi, i: (0,) * a.ndim)
    cb = conv_b.reshape(1, LRU_W)
    out_f = pl.BlockSpec((1, t, LRU_W), lambda bi, i: (bi, i, 0))
    out_r = pl.BlockSpec((1, t, LRU_W), lambda bi, i: (bi, n_tiles - 1 - i, 0))
    return pl.pallas_call(
        functools.partial(_lru_kernel, n_tiles=n_tiles),
        name="rglru",
        out_shape=[jax.ShapeDtypeStruct((b, s, LRU_W), F32)] * 2,
        grid=(b, n_tiles),
        in_specs=_halo_specs(LRU_W, t, n_tiles, s, False) + _halo_specs(LRU_W, t, n_tiles, s, True)
        + [full(conv_w), full(cb), full(wg), full(bg), full(lam)],
        out_specs=[out_f, out_r],
        scratch_shapes=[pltpu.VMEM((SUBLANES_V7X, LRU_W), F32)] * 2,
        compiler_params=_cparams(("parallel", "arbitrary")),
    )(xb, xb, xb, xb, xb, xb, conv_w, cb, wg, bg, lam)


def _lru_gate_weights(w_a, b_a, w_x, b_x, lam):
    def bd(w):
        eye = jnp.eye(LRU_BLOCKS, dtype=w.dtype)
        return jnp.einsum("dnij,nm->dnimj", w, eye).reshape(2, LRU_W, LRU_W)
    wg = jnp.concatenate([bd(w_a), bd(w_x)], axis=-1).astype(BF16)
    bg = jnp.concatenate([b_a, b_x], axis=-1).reshape(2, 1, 2 * LRU_W)
    return wg, bg, lam.reshape(2, 1, LRU_W)


def _rwkv_prep_kernel(z_ref, zp_ref, zn_ref, mu_ref, ones_ref, wup_ref, aup_ref, gup_ref, w0_ref, a0_ref,
                      kk_w_ref, ka_ref, rk_ref,
                      r_ref, v_ref, kk_ref, g_ref, bonus_ref, lw0_ref, kd0_ref, b0_ref, lw1_ref, kd1_ref, b1_ref,
                      *, n_tiles):
    i = pl.program_id(1)
    z = z_ref[0]
    t = z.shape[0]
    row = lax.broadcasted_iota(jnp.int32, z.shape, 0)
    pv = jnp.where(i > 0, zp_ref[0], 0.0)
    nx = jnp.where(i < n_tiles - 1, zn_ref[0], 0.0)
    prev = jnp.where(row == 0, pv[7:8], _shift_rows(z, 1))
    nxt = jnp.where(row == t - 1, nx[0:1], _shift_rows(z, -1))
    f = z + mu_ref[...] * (0.5 * (prev + nxt) - z)

    w = RWKV_W
    r = f[:, 0:w]
    k = f[:, w:2 * w]
    v = f[:, 2 * w:3 * w]
    lora = f[:, 3 * w:3 * w + W_LORA + A_LORA]
    xg = f[:, 3 * w + W_LORA + A_LORA:]
    ones = ones_ref[...]

    g_ref[0] = _dot(_sigmoid(xg), gup_ref[...])
    kk = k * kk_w_ref[...]
    ssq = _dot_split(kk * kk, ones, 2)
    kk = kk * lax.rsqrt(jnp.maximum(ssq, 1e-24))
    a_lora = _dot(lora, aup_ref[...])
    w_lora = jnp.tanh(lora)
    r_ref[0] = r
    v_ref[0] = v
    kk_ref[0] = kk
    bonus_ref[0] = _dot_split(r * k * rk_ref[...], ones, 2) * v
    ka = ka_ref[...]
    for d, (lw_ref, kd_ref, b_ref) in enumerate(((lw0_ref, kd0_ref, b0_ref), (lw1_ref, kd1_ref, b1_ref))):
        u = w0_ref[d] + _dot(w_lora, wup_ref[d])
        lw_ref[0] = -jnp.exp(-_softplus(-u) - 0.5)
        a = _sigmoid(a0_ref[d] + a_lora)
        kd_ref[0] = k * (1.0 + (a - 1.0) * ka)
        b_ref[0] = kk * a


def _rwkv_prep(zr, mu, w_up, w0, a_up, a0, g_up, k_k, k_a, r_k):
    b, s, _ = zr.shape
    t = min(SEQ_TILE, s)
    n_tiles = s // t
    full = lambda a: pl.BlockSpec(a.shape, lambda bi, i: (0,) * a.ndim)
    zeros = jnp.zeros((2, W_LORA, RWKV_W), F32)
    wup = jnp.concatenate([w_up, zeros], axis=1).astype(BF16)
    aup = jnp.concatenate([jnp.zeros((W_LORA, RWKV_W), F32), a_up], axis=0).astype(BF16)
    consts = [mu.reshape(1, RWKV_IN), _block_ones(RWKV_W, RWKV_HD), wup, aup, g_up.astype(BF16),
              w0.reshape(2, 1, RWKV_W), a0.reshape(2, 1, RWKV_W), k_k.reshape(1, RWKV_W),
              k_a.reshape(1, RWKV_W), r_k.reshape(1, RWKV_W)]
    out = pl.BlockSpec((1, t, RWKV_W), lambda bi, i: (bi, i, 0))
    return pl.pallas_call(
        functools.partial(_rwkv_prep_kernel, n_tiles=n_tiles),
        name="rwkv_prep",
        out_shape=[jax.ShapeDtypeStruct((b, s, RWKV_W), F32)] * 11,
        grid=(b, n_tiles),
        in_specs=_halo_specs(RWKV_IN, t, n_tiles, s, False) + [full(c) for c in consts],
        out_specs=[out] * 11,
        compiler_params=_cparams(("parallel", "parallel")),
    )(zr, zr, zr, *consts)


def _rwkv_direction(r_ref, v_ref, kk_ref, lw_ref, kd_ref, b_ref, y_ref, st_ref, rev, chunk):
    t = r_ref.shape[1]
    c = chunk
    hc = RWKV_HEADS * c
    rows = lax.broadcasted_iota(jnp.int32, (hc, hc), 0)
    cols = lax.broadcasted_iota(jnp.int32, (hc, hc), 1)
    lane = lax.broadcasted_iota(jnp.int32, (hc, RWKV_W), 1)
    srow = lax.broadcasted_iota(jnp.int32, (hc, RWKV_W), 0)
    head_mask = (srow // c) == (lane // RWKV_HD)
    ti = lax.broadcasted_iota(jnp.int32, (c, c), 0)
    tj = lax.broadcasted_iota(jnp.int32, (c, c), 1)
    if rev:
        strict, incl, tri, last = cols > rows, cols >= rows, tj >= ti, 0
    else:
        strict, incl, tri, last = cols < rows, cols <= rows, tj <= ti, c - 1
    tri = jnp.where(tri, 1.0, 0.0).astype(BF16)
    eye = rows == cols

    def stack(x):
        return jnp.where(head_mask, jnp.concatenate([x] * RWKV_HEADS, axis=0), 0.0)

    order = range(t // c)
    for ci in (reversed(order) if rev else order):
        sl = pl.ds(ci * c, c)
        r, v, kk, lw, kd, b = (ref[0, sl, :] for ref in (r_ref, v_ref, kk_ref, lw_ref, kd_ref, b_ref))
        cum = _tri_cumsum(tri, lw)
        excl = cum - lw
        inv_decay = jnp.exp(-cum)
        a_t = stack(-kk * jnp.exp(excl)).astype(BF16)
        r_t_f = stack(r * jnp.exp(cum))
        r_t = r_t_f.astype(BF16)
        cum_last = cum[last:last + 1]
        to_end = jnp.exp(cum_last - cum)
        bk = jnp.concatenate([stack(b * inv_decay), stack(kd * inv_decay)], axis=0).astype(BF16)
        v_s = stack(v).astype(BF16)
        b_end = stack(b * to_end).astype(BF16)
        k_end = stack(kd * to_end).astype(BF16)

        g1 = _dot_nt(a_t, bk)
        a_ab = jnp.where(strict, g1[:, :hc], 0.0)
        a_ak = jnp.where(strict, g1[:, hc:], 0.0)
        g2 = _dot_nt(r_t, bk)
        a_rb = jnp.where(incl, g2[:, :hc], 0.0).astype(BF16)
        a_rk = jnp.where(incl, g2[:, hc:], 0.0).astype(BF16)

        tinv = jnp.where(eye, 1.0, a_ab)
        power = a_ab
        span = 1
        while 2 * span < c:
            power = _dot(power, power)
            tinv = tinv + _dot(tinv, power)
            span *= 2

        w1 = _dot(a_ak, v_s)
        x = _dot(tinv, jnp.concatenate([a_t, w1.astype(BF16)], axis=1)).astype(BF16)
        z = _dot(a_rb, x)
        r_bar = r_t_f + z[:, :RWKV_W]
        y0 = z[:, RWKV_W:] + _dot(a_rk, v_s)
        gh = _dot_tn(b_end, x)
        trans = jnp.where(eye, jnp.exp(cum_last), 0.0) + gh[:, :RWKV_W]
        h0 = gh[:, RWKV_W:] + _dot_tn(k_end, v_s)

        st = st_ref[...]
        y_s = _dot(r_bar, st) + y0
        st_ref[...] = _dot(trans, st) + h0
        y = y_s[0:c]
        for hh in range(1, RWKV_HEADS):
            y = y + y_s[hh * c:(hh + 1) * c]
        y_ref[0, sl, :] = y


def _tri_cumsum(tri, x):
    acc = None
    rem = x
    for _ in range(3):
        hi = rem.astype(BF16)
        part = jnp.dot(tri, hi, preferred_element_type=F32)
        acc = part if acc is None else acc + part
        rem = rem - hi.astype(F32)
    return acc


def _rwkv_scan_kernel(rf_ref, vf_ref, kkf_ref, lwf_ref, kdf_ref, bf_ref, rr_ref, vr_ref, kkr_ref, lwr_ref, kdr_ref,
                      br_ref, yf_ref, yr_ref, sf_ref, sr_ref, *, chunk):
    @pl.when(pl.program_id(1) == 0)
    def _():
        sf_ref[...] = jnp.zeros_like(sf_ref)
        sr_ref[...] = jnp.zeros_like(sr_ref)

    _rwkv_direction(rf_ref, vf_ref, kkf_ref, lwf_ref, kdf_ref, bf_ref, yf_ref, sf_ref, False, chunk)
    _rwkv_direction(rr_ref, vr_ref, kkr_ref, lwr_ref, kdr_ref, br_ref, yr_ref, sr_ref, True, chunk)


def _rwkv_scan(r, v, kk, lw0, kd0, b0, lw1, kd1, b1):
    b, s, _ = r.shape
    t = min(SEQ_TILE, s)
    chunk = min(RWKV_CHUNK, t)
    n_tiles = s // t
    fwd = pl.BlockSpec((1, t, RWKV_W), lambda bi, i: (bi, i, 0))
    bwd = pl.BlockSpec((1, t, RWKV_W), lambda bi, i: (bi, n_tiles - 1 - i, 0))
    return pl.pallas_call(
        functools.partial(_rwkv_scan_kernel, chunk=chunk),
        name="rwkv_scan",
        out_shape=[jax.ShapeDtypeStruct((b, s, RWKV_W), F32)] * 2,
        grid=(b, n_tiles),
        in_specs=[fwd] * 6 + [bwd] * 6,
        out_specs=[fwd, bwd],
        scratch_shapes=[pltpu.VMEM((RWKV_W, RWKV_W), F32)] * 2,
        compiler_params=_cparams(("parallel", "arbitrary")),
    )(r, v, kk, lw0, kd0, b0, r, v, kk, lw1, kd1, b1)


def _mixout_kernel(x_ref, mod_ref, hf_ref, hr_ref, yb_ref, yf_ref, yr_ref, g_ref, bonus_ref, oa_ref,
                   wl_ref, wr_ref, wa_ref, ones_ref, lng_ref, lnb_ref, o_ref):
    o_lru = (hf_ref[...] + hr_ref[...]) * _gelu_tanh(yb_ref[...])
    y = yf_ref[...] + yr_ref[...]
    ones = ones_ref[...]
    mu = _dot_split(y, ones, 3) * (1.0 / RWKV_HD)
    dlt = y - mu
    var = _dot_split(dlt * dlt, ones, 2) * (1.0 / RWKV_HD)
    yn = dlt * lax.rsqrt(var + GN_EPS) * lng_ref[...] + lnb_ref[...]
    o_rwkv = (yn + bonus_ref[...]) * g_ref[...]
    acc = _dot(o_lru, wl_ref[...]) + _dot(o_rwkv, wr_ref[...])
    for hh in range(ATT_HEADS):
        acc = acc + jnp.dot(oa_ref[0, hh], wa_ref[hh], preferred_element_type=F32)
    o_ref[...] = x_ref[...] + mod_ref[0, 5:6, :] * acc


def _mixout(x, mod, hf, hr, yb, yf, yr, g, bonus, oa, w_out, ln_g, ln_b):
    b, s, _ = x.shape
    n = b * s
    tm = min(MIX_TOKENS, s)
    nt = s // tm
    tok = lambda w: pl.BlockSpec((tm, w), lambda i: (i, 0))
    full = lambda a: pl.BlockSpec(a.shape, lambda i: (0,) * a.ndim)
    wl = w_out[:LRU_W]
    wr = w_out[LRU_W:LRU_W + RWKV_W]
    wa = w_out[LRU_W + RWKV_W:].reshape(ATT_HEADS, ATT_HD, D_MODEL)
    ones = _block_ones(RWKV_W, RWKV_HD)
    lng = ln_g.reshape(1, RWKV_W)
    lnb = ln_b.reshape(1, RWKV_W)
    flat = lambda a: a.reshape(n, a.shape[-1])
    out = pl.pallas_call(
        _mixout_kernel,
        name="mix_out",
        out_shape=jax.ShapeDtypeStruct((n, D_MODEL), F32),
        grid=(n // tm,),
        in_specs=[
            tok(D_MODEL),
            pl.BlockSpec((1, N_MOD, D_MODEL), lambda i: ((i * tm) // s, 0, 0)),
            tok(LRU_W), tok(LRU_W), tok(LRU_W), tok(RWKV_W), tok(RWKV_W), tok(RWKV_W), tok(RWKV_W),
            pl.BlockSpec((1, ATT_HEADS, tm, ATT_HD), lambda i: (i // nt, 0, i % nt, 0)),
            full(wl), full(wr), full(wa), full(ones), full(lng), full(lnb),
        ],
        out_specs=tok(D_MODEL),
        compiler_params=_cparams(("parallel",)),
    )(flat(x), mod, flat(hf), flat(hr), flat(yb), flat(yf), flat(yr), flat(g), flat(bonus), oa,
      wl, wr, wa, ones, lng, lnb)
    return out.reshape(b, s, D_MODEL)


def _layer(x, mod, lp, rope):
    x = _ffn(x, mod, 0, lp["norm_g"][0], lp["ffn_w_in"][0], lp["ffn_w_out"][0])
    xb, yb, zr, q, k, v = _mixin(x, mod, lp["norm_g"][1], lp["w_mix_in"], lp["attn_q_norm"], lp["attn_k_norm"], rope)
    hf, hr = _lru(xb, lp["lru_conv_w"], lp["lru_conv_b"], *lp["lru_gates"])
    r, vv, kk, g, bonus, lw0, kd0, b0, lw1, kd1, b1 = _rwkv_prep(
        zr, lp["rwkv_mu"], lp["rwkv_w_up"], lp["rwkv_w0"], lp["rwkv_a_up"], lp["rwkv_a0"], lp["rwkv_g_up"],
        lp["rwkv_k_k"], lp["rwkv_k_a"], lp["rwkv_r_k"])
    yf, yr = _rwkv_scan(r, vv, kk, lw0, kd0, b0, lw1, kd1, b1)
    oa = _attention(q, k, v)
    x = _mixout(x, mod, hf, hr, yb, yf, yr, g, bonus, oa, lp["w_mix_out"], lp["rwkv_ln_g"], lp["rwkv_ln_b"])
    x = _ffn(x, mod, 2, lp["norm_g"][2], lp["ffn_w_in"][1], lp["ffn_w_out"][1])
    return x


def kernel(x_prompt, x_sample, c_prompt, c_sample, w_ada, b_ada, norm_g, ffn_w_in, ffn_w_out, w_mix_in, w_mix_out, lru_conv_w, lru_conv_b, lru_w_gate_a, lru_b_gate_a, lru_w_gate_x, lru_b_gate_x, lru_lambda, rwkv_mu, rwkv_w_up, rwkv_w0, rwkv_a_up, rwkv_a0, rwkv_g_up, rwkv_k_k, rwkv_k_a, rwkv_r_k, rwkv_ln_g, rwkv_ln_b, attn_q_norm, attn_k_norm):
    depth = w_ada.shape[0]
    bp, bs = x_prompt.shape[0], x_sample.shape[0]
    rows = -(-(bp + bs) // SUBLANES_V7X) * SUBLANES_V7X
    c_all = jnp.concatenate([c_prompt, c_sample, jnp.zeros((rows - bp - bs, D_MODEL), F32)], axis=0)
    mod_all = _ada_mod(c_all, w_ada, b_ada)

    params = {
        "norm_g": norm_g, "ffn_w_in": ffn_w_in.astype(BF16), "ffn_w_out": ffn_w_out.astype(BF16),
        "w_mix_in": w_mix_in.astype(BF16), "w_mix_out": w_mix_out.astype(BF16),
        "lru_conv_w": lru_conv_w, "lru_conv_b": lru_conv_b, "rwkv_mu": rwkv_mu, "rwkv_w_up": rwkv_w_up,
        "rwkv_w0": rwkv_w0, "rwkv_a_up": rwkv_a_up, "rwkv_a0": rwkv_a0, "rwkv_g_up": rwkv_g_up,
        "rwkv_k_k": rwkv_k_k, "rwkv_k_a": rwkv_k_a, "rwkv_r_k": rwkv_r_k, "rwkv_ln_g": rwkv_ln_g,
        "rwkv_ln_b": rwkv_ln_b, "attn_q_norm": attn_q_norm, "attn_k_norm": attn_k_norm,
    }
    outs = []
    for x, lo, nb in ((x_prompt, 0, bp), (x_sample, bp, bs)):
        rope = _rope_tables(x.shape[1])
        for l in range(depth):
            lp = {name: arr[l] for name, arr in params.items()}
            lp["lru_gates"] = _lru_gate_weights(lru_w_gate_a[l], lru_b_gate_a[l], lru_w_gate_x[l],
                                                lru_b_gate_x[l], lru_lambda[l])
            mod = mod_all[l, lo:lo + nb].reshape(nb, N_MOD, D_MODEL)
            x = _layer(x, mod, lp, rope)
        outs.append(x)
    return tuple(outs)
```

```python
import functools
import math

import jax
import jax.numpy as jnp
from jax import lax
from jax.experimental import pallas as pl
from jax.experimental.pallas import tpu as pltpu

F32 = jnp.float32
BF16 = jnp.bfloat16

D_MODEL = 1024
D_FF = 2816
N_MOD = 9
GRID_W = 64
LRU_W = 384
LRU_BLOCKS = 6
LRU_BS = LRU_W // LRU_BLOCKS
LRU_C = 8.0
RWKV_HEADS = 4
RWKV_HD = 64
RWKV_W = RWKV_HEADS * RWKV_HD
W_LORA = 64
A_LORA = 64
G_LORA = 128
RWKV_IN = 3 * RWKV_W + W_LORA + A_LORA + G_LORA
ATT_HEADS = 6
ATT_KV = 2
ATT_G = ATT_HEADS // ATT_KV
ATT_HD = 64
ATT_Q = ATT_HEADS * ATT_HD
ATT_KVW = ATT_KV * ATT_HD
ROPE_THETA = 10000.0
ROPE_PAIRS = ATT_HD // 4
D_MIX = LRU_W + RWKV_W + ATT_Q
D_IN = 2 * LRU_W + RWKV_IN + ATT_Q + 2 * ATT_KVW
NORM_EPS = 1e-6
GN_EPS = 64e-5

LANES_V7X = 128
SUBLANES_V7X = 8
VMEM_LIMIT_BYTES = 56 * 1024 * 1024

FFN_TOKENS = 512
FFN_COLS = 256
MIX_TOKENS = 512
SEQ_TILE = 256
RWKV_CHUNK = 64
ATT_Q_TILE = 256
ATT_KV_TILE = 1024
ATT_KV_SUB = 256
ATT_VT_ROWS = 80
LOG2_E = 1.4426950408889634


def _cparams(sem):
    return pltpu.CompilerParams(dimension_semantics=sem, vmem_limit_bytes=VMEM_LIMIT_BYTES)


def _dot(a, b):
    return jnp.dot(a.astype(BF16), b.astype(BF16), preferred_element_type=F32)


def _dot_nt(a, b):
    return lax.dot_general(a.astype(BF16), b.astype(BF16), (((1,), (1,)), ((), ())),
                           preferred_element_type=F32)


def _dot_tn(a, b):
    return lax.dot_general(a.astype(BF16), b.astype(BF16), (((0,), (0,)), ((), ())),
                           preferred_element_type=F32)


def _dot_split(x, w, terms):
    acc = None
    rem = x
    for _ in range(terms):
        hi = rem.astype(BF16)
        part = jnp.dot(hi, w, preferred_element_type=F32)
        acc = part if acc is None else acc + part
        rem = rem - hi.astype(F32)
    return acc


def _sigmoid(x):
    return 1.0 / (1.0 + jnp.exp(-x))


def _silu(x):
    return x * _sigmoid(x)


def _softplus(x):
    return jnp.maximum(x, 0.0) + jnp.log(1.0 + jnp.exp(-jnp.abs(x)))


def _gelu_tanh(x):
    c = math.sqrt(2.0 / math.pi)
    return 0.5 * x * (1.0 + jnp.tanh(c * (x + 0.044715 * (x * x * x))))


def _rms_mod(x, g, scale, shift):
    ms = jnp.mean(x * x, axis=-1, keepdims=True)
    y = x * lax.rsqrt(ms + NORM_EPS) * g
    return y * (1.0 + scale) + shift


def _ada_kernel(c_ref, w_ref, b_ref, o_ref):
    c = c_ref[...]
    o_ref[0] = _dot(_silu(c), w_ref[0]) + b_ref[0]


def _ada_mod(c_all, w_ada, b_ada):
    depth = w_ada.shape[0]
    rows = c_all.shape[0]
    ncol = N_MOD * D_MODEL
    tn = 1152
    return pl.pallas_call(
        _ada_kernel,
        name="ada_mod",
        out_shape=jax.ShapeDtypeStruct((depth, rows, ncol), F32),
        grid=(depth, ncol // tn),
        in_specs=[
            pl.BlockSpec((rows, D_MODEL), lambda l, j: (0, 0)),
            pl.BlockSpec((1, D_MODEL, tn), lambda l, j: (l, 0, j)),
            pl.BlockSpec((1, 1, tn), lambda l, j: (l, 0, j)),
        ],
        out_specs=pl.BlockSpec((1, rows, tn), lambda l, j: (l, 0, j)),
        compiler_params=_cparams(("parallel", "parallel")),
    )(c_all, w_ada, b_ada.reshape(depth, 1, ncol))


def _ffn_kernel(x_ref, mod_ref, g_ref, wi_ref, wo_ref, o_ref, *, sub, tf):
    x = x_ref[...]
    h = _rms_mod(x, g_ref[...], mod_ref[0, 3 * sub + 1:3 * sub + 2, :], mod_ref[0, 3 * sub:3 * sub + 1, :])
    h = h.astype(BF16)
    n_chunks = D_FF // tf

    def gate_up(c):
        return (jnp.dot(h, wi_ref[:, c * tf:(c + 1) * tf], preferred_element_type=F32),
                jnp.dot(h, wi_ref[:, D_FF + c * tf:D_FF + (c + 1) * tf], preferred_element_type=F32))

    pending = gate_up(0)
    acc = None
    for c in range(n_chunks):
        gate, up = pending
        if c + 1 < n_chunks:
            pending = gate_up(c + 1)
        act = (_silu(gate) * up).astype(BF16)
        part = jnp.dot(act, wo_ref[c * tf:(c + 1) * tf, :], preferred_element_type=F32)
        acc = part if acc is None else acc + part
    o_ref[...] = x + 0.5 * mod_ref[0, 3 * sub + 2:3 * sub + 3, :] * acc


def _resident(a):
    return pl.BlockSpec(a.shape, lambda *_: (0,) * a.ndim, pipeline_mode=pl.Buffered(1))


def _ffn(x, mod, sub, g, w_in, w_out):
    b, s, _ = x.shape
    n = b * s
    tm = min(FFN_TOKENS, s)
    g2 = g.reshape(1, D_MODEL)
    out = pl.pallas_call(
        functools.partial(_ffn_kernel, sub=sub, tf=FFN_COLS),
        name="ffn",
        out_shape=jax.ShapeDtypeStruct((n, D_MODEL), F32),
        grid=(n // tm,),
        in_specs=[
            pl.BlockSpec((tm, D_MODEL), lambda i: (i, 0)),
            pl.BlockSpec((1, N_MOD, D_MODEL), lambda i: ((i * tm) // s, 0, 0)),
            _resident(g2), _resident(w_in), _resident(w_out),
        ],
        out_specs=pl.BlockSpec((tm, D_MODEL), lambda i: (i, 0)),
        compiler_params=_cparams(("parallel",)),
    )(x.reshape(n, D_MODEL), mod, g2, w_in, w_out)
    return out.reshape(b, s, D_MODEL)


def _rope_rotate(x, cos, sin_signed):
    outs = []
    for c in range(x.shape[1] // LANES_V7X):
        sl = slice(c * LANES_V7X, (c + 1) * LANES_V7X)
        xc = x[:, sl]
        lane = lax.broadcasted_iota(jnp.int32, xc.shape, 1)
        first = (lane % 32) < 16
        partner = jnp.where(first, pltpu.roll(xc, LANES_V7X - 16, 1), pltpu.roll(xc, 16, 1))
        outs.append(xc * cos[:, sl] + partner * sin_signed[:, sl])
    return jnp.concatenate(outs, axis=1) if len(outs) > 1 else outs[0]


def _mixin_kernel(x_ref, mod_ref, g_ref, w_ref, onesq_ref, gq_ref, gk_ref, cq_ref, sq_ref, ck_ref, sk_ref,
                  xb_ref, yb_ref, zr_ref, q_ref, k_ref, v_ref):
    h = _rms_mod(x_ref[...], g_ref[...], mod_ref[0, 4:5, :], mod_ref[0, 3:4, :]).astype(BF16)

    def proj(lo, hi):
        return jnp.dot(h, w_ref[:, lo:hi], preferred_element_type=F32)

    o1 = 2 * LRU_W + RWKV_IN
    xb_ref[...] = proj(0, LRU_W)
    yb_ref[...] = proj(LRU_W, 2 * LRU_W)
    zr_ref[...] = proj(2 * LRU_W, o1)
    q = proj(o1, o1 + ATT_Q)
    k = proj(o1 + ATT_Q, o1 + ATT_Q + ATT_KVW)
    v = proj(o1 + ATT_Q + ATT_KVW, D_IN)

    ones = onesq_ref[...]
    ssq_q = _dot_split(q * q, ones, 2)
    qn = q * lax.rsqrt(ssq_q * (1.0 / ATT_HD) + NORM_EPS) * gq_ref[...]
    qr = _rope_rotate(qn, cq_ref[...], sq_ref[...]) * (ATT_HD ** -0.5 * LOG2_E)
    ssq_k = _dot_split(k * k, ones[:ATT_KVW, :ATT_KVW], 2)
    kn = k * lax.rsqrt(ssq_k * (1.0 / ATT_HD) + NORM_EPS) * gk_ref[...]
    kr = _rope_rotate(kn, ck_ref[...], sk_ref[...])
    for hh in range(ATT_HEADS):
        q_ref[0, hh] = qr[:, hh * ATT_HD:(hh + 1) * ATT_HD].astype(BF16)
    for hh in range(ATT_KV):
        k_ref[0, hh] = kr[:, hh * ATT_HD:(hh + 1) * ATT_HD].astype(BF16)
        v_ref[0, hh] = v[:, hh * ATT_HD:(hh + 1) * ATT_HD].astype(BF16)


def _rope_tables(seq):
    n_rows = seq // GRID_W
    row = jnp.repeat(jnp.arange(n_rows, dtype=F32), GRID_W)
    col = jnp.tile(jnp.arange(GRID_W, dtype=F32), n_rows)
    inv = ROPE_THETA ** (-jnp.arange(ROPE_PAIRS, dtype=F32) / ROPE_PAIRS)
    ang_r = row[:, None] * inv
    ang_c = col[:, None] * inv
    cos_h = jnp.concatenate([jnp.cos(ang_r), jnp.cos(ang_r), jnp.cos(ang_c), jnp.cos(ang_c)], axis=1)
    sin_h = jnp.concatenate([-jnp.sin(ang_r), jnp.sin(ang_r), -jnp.sin(ang_c), jnp.sin(ang_c)], axis=1)
    return (jnp.tile(cos_h, (1, ATT_HEADS)), jnp.tile(sin_h, (1, ATT_HEADS)),
            jnp.tile(cos_h, (1, ATT_KV)), jnp.tile(sin_h, (1, ATT_KV)))


def _block_ones(width, block):
    idx = jnp.arange(width)
    return (idx[:, None] // block == idx[None, :] // block).astype(BF16)


def _mixin(x, mod, g, w_in, gq, gk, rope):
    b, s, _ = x.shape
    n = b * s
    tm = min(MIX_TOKENS, s)
    nt = s // tm
    cq, sq, ck, sk = rope
    tok = lambda w: pl.BlockSpec((tm, w), lambda i: (i, 0))
    pos = lambda w: pl.BlockSpec((tm, w), lambda i: (i % nt, 0))
    full = lambda a: pl.BlockSpec(a.shape, lambda i: (0,) * a.ndim)
    ones = _block_ones(ATT_Q, ATT_HD)
    gq_t = jnp.tile(gq.reshape(1, ATT_HD), (1, ATT_HEADS))
    gk_t = jnp.tile(gk.reshape(1, ATT_HD), (1, ATT_KV))
    g2 = g.reshape(1, D_MODEL)
    head = lambda nh: pl.BlockSpec((1, nh, tm, ATT_HD), lambda i: (i // nt, 0, i % nt, 0))
    outs = pl.pallas_call(
        _mixin_kernel,
        name="mix_in",
        out_shape=[
            jax.ShapeDtypeStruct((n, LRU_W), F32),
            jax.ShapeDtypeStruct((n, LRU_W), F32),
            jax.ShapeDtypeStruct((n, RWKV_IN), F32),
            jax.ShapeDtypeStruct((b, ATT_HEADS, s, ATT_HD), BF16),
            jax.ShapeDtypeStruct((b, ATT_KV, s, ATT_HD), BF16),
            jax.ShapeDtypeStruct((b, ATT_KV, s, ATT_HD), BF16),
        ],
        grid=(n // tm,),
        in_specs=[
            tok(D_MODEL),
            pl.BlockSpec((1, N_MOD, D_MODEL), lambda i: ((i * tm) // s, 0, 0)),
            full(g2), full(w_in), full(ones), full(gq_t), full(gk_t),
            pos(ATT_Q), pos(ATT_Q), pos(ATT_KVW), pos(ATT_KVW),
        ],
        out_specs=[tok(LRU_W), tok(LRU_W), tok(RWKV_IN), head(ATT_HEADS), head(ATT_KV), head(ATT_KV)],
        compiler_params=_cparams(("parallel",)),
    )(x.reshape(n, D_MODEL), mod, g2, w_in, ones, gq_t, gk_t, cq, sq, ck, sk)
    xb, yb, zr, q, k, v = outs
    return (xb.reshape(b, s, LRU_W), yb.reshape(b, s, LRU_W), zr.reshape(b, s, RWKV_IN), q, k, v)


def _attn_kernel(q_ref, k_ref, vt_ref, o_ref, m_ref, acc_ref, *, tq, nk):
    ki = pl.program_id(2)

    @pl.when(ki == 0)
    def _():
        m_ref[...] = jnp.full_like(m_ref, -jnp.inf)
        acc_ref[...] = jnp.zeros_like(acc_ref)

    tk = k_ref.shape[2]
    sub = min(ATT_KV_SUB, tk)
    qs = [q_ref[0, ATT_G * g:ATT_G * (g + 1)].reshape(ATT_G * tq, ATT_HD) for g in range(ATT_KV)]
    jobs = [(g, c * sub) for c in range(tk // sub) for g in range(ATT_KV)]

    def scores(job):
        g, lo = job
        return lax.dot_general(k_ref[0, g, lo:lo + sub, :], qs[g], (((1,), (1,)), ((), ())),
                               preferred_element_type=F32)

    ahead = 2
    pending = {i: scores(jobs[i]) for i in range(min(ahead, len(jobs)))}
    for i, (g, lo) in enumerate(jobs):
        if i + ahead < len(jobs):
            pending[i + ahead] = scores(jobs[i + ahead])
        st = pending.pop(i)
        m_prev = m_ref[g]
        m_new = jnp.maximum(m_prev, jnp.max(st, axis=0, keepdims=True))
        p = jnp.exp2(st - m_new).astype(BF16)
        acc_ref[g] = jnp.exp2(m_prev - m_new) * acc_ref[g] + jnp.dot(vt_ref[0, g, :, lo:lo + sub], p,
                                                                      preferred_element_type=F32)
        m_ref[g] = m_new

    @pl.when(ki == nk - 1)
    def _():
        for g in range(ATT_KV):
            acc = acc_ref[g]
            o = (acc[:ATT_HD] / acc[ATT_HD:ATT_HD + 1]).T
            o_ref[0, ATT_G * g:ATT_G * (g + 1)] = o.reshape(ATT_G, tq, ATT_HD).astype(BF16)


def _attention(q, k, v):
    b, _, s, _ = q.shape
    ones_rows = jnp.ones((b, ATT_KV, ATT_VT_ROWS - ATT_HD, s), BF16)
    vt = jnp.concatenate([jnp.swapaxes(v, 2, 3), ones_rows], axis=2)
    tq = min(ATT_Q_TILE, s)
    tk = min(ATT_KV_TILE, s)
    nk = s // tk
    return pl.pallas_call(
        functools.partial(_attn_kernel, tq=tq, nk=nk),
        name="attention",
        out_shape=jax.ShapeDtypeStruct((b, ATT_HEADS, s, ATT_HD), BF16),
        grid=(b, s // tq, nk),
        in_specs=[
            pl.BlockSpec((1, ATT_HEADS, tq, ATT_HD), lambda bi, qi, ki: (bi, 0, qi, 0)),
            pl.BlockSpec((1, ATT_KV, tk, ATT_HD), lambda bi, qi, ki: (bi, 0, ki, 0)),
            pl.BlockSpec((1, ATT_KV, ATT_VT_ROWS, tk), lambda bi, qi, ki: (bi, 0, 0, ki)),
        ],
        out_specs=pl.BlockSpec((1, ATT_HEADS, tq, ATT_HD), lambda bi, qi, ki: (bi, 0, qi, 0)),
        scratch_shapes=[
            pltpu.VMEM((ATT_KV, 1, ATT_G * tq), F32),
            pltpu.VMEM((ATT_KV, ATT_VT_ROWS, ATT_G * tq), F32),
        ],
        compiler_params=_cparams(("parallel", "parallel", "arbitrary")),
    )(q, k, vt)


def _shift_rows(x, shift):
    return pltpu.roll(x, shift % x.shape[0], 0)


def _lru_direction(x_ref, xp_ref, xn_ref, cw_ref, cb_ref, wg_ref, bg_ref, lam_ref, h_ref, carry_ref,
                   tile, n_tiles, rev):
    x = x_ref[0]
    t = x.shape[0]
    row = lax.broadcasted_iota(jnp.int32, x.shape, 0)
    pv = jnp.where(tile > 0, xp_ref[0], 0.0)
    nx = jnp.where(tile < n_tiles - 1, xn_ref[0], 0.0)
    xm1 = jnp.where(row == 0, pv[7:8], _shift_rows(x, 1))
    xm2 = jnp.where(row == 0, pv[6:7], jnp.where(row == 1, pv[7:8], _shift_rows(x, 2)))
    xp1 = jnp.where(row == t - 1, nx[0:1], _shift_rows(x, -1))
    cw = cw_ref[...]
    xc = cw[0:1] * xm2 + cw[1:2] * xm1 + cw[2:3] * x + cw[3:4] * xp1 + cb_ref[...]

    gates = _dot(xc, wg_ref[...]) + bg_ref[...]
    r = _sigmoid(gates[:, :LRU_W])
    gi = _sigmoid(gates[:, LRU_W:])
    log_a = (-LRU_C) * r * _softplus(-lam_ref[...])
    a = jnp.exp(log_a)
    u = jnp.sqrt(1.0 - a * a) * (gi * xc)

    step = 1
    while step < t:
        if not rev:
            valid = row >= step
            a_sh = jnp.where(valid, _shift_rows(a, step), 1.0)
            u_sh = jnp.where(valid, _shift_rows(u, step), 0.0)
        else:
            valid = row < t - step
            a_sh = jnp.where(valid, _shift_rows(a, -step), 1.0)
            u_sh = jnp.where(valid, _shift_rows(u, -step), 0.0)
        u = u + a * u_sh
        a = a * a_sh
        step *= 2
    h = u + a * carry_ref[0:1, :]
    h_ref[0] = h
    edge = h[0:1] if rev else h[t - 1:t]
    carry_ref[...] = jnp.broadcast_to(edge, carry_ref.shape)


def _lru_kernel(xf_ref, xfp_ref, xfn_ref, xr_ref, xrp_ref, xrn_ref, cw_ref, cb_ref, wg_ref, bg_ref, lam_ref,
                hf_ref, hr_ref, cf_ref, cr_ref, *, n_tiles):
    i = pl.program_id(1)

    @pl.when(i == 0)
    def _():
        cf_ref[...] = jnp.zeros_like(cf_ref)
        cr_ref[...] = jnp.zeros_like(cr_ref)

    _lru_direction(xf_ref, xfp_ref, xfn_ref, cw_ref, cb_ref, wg_ref.at[0], bg_ref.at[0], lam_ref.at[0],
                   hf_ref, cf_ref, i, n_tiles, False)
    _lru_direction(xr_ref, xrp_ref, xrn_ref, cw_ref, cb_ref, wg_ref.at[1], bg_ref.at[1], lam_ref.at[1],
                   hr_ref, cr_ref, n_tiles - 1 - i, n_tiles, True)


def _halo_specs(width, t, n_tiles, s, rev):
    r8 = t // SUBLANES_V7X
    last8 = s // SUBLANES_V7X - 1
    tile = (lambda i: n_tiles - 1 - i) if rev else (lambda i: i)
    return [
        pl.BlockSpec((1, t, width), lambda b, i: (b, tile(i), 0)),
        pl.BlockSpec((1, SUBLANES_V7X, width), lambda b, i: (b, jnp.maximum(tile(i) * r8 - 1, 0), 0)),
        pl.BlockSpec((1, SUBLANES_V7X, width), lambda b, i: (b, jnp.minimum((tile(i) + 1) * r8, last8), 0)),
    ]


def _lru(xb, conv_w, conv_b, wg, bg, lam):
    b, s, _ = xb.shape
    t = min(SEQ_TILE, s)
    n_tiles = s // t
    full = lambda a: pl.BlockSpec(a.shape, lambda bi, i: (0,) * a.ndim)
    cb = conv_b.reshape(1, LRU_W)
    out_f = pl.BlockSpec((1, t, LRU_W), lambda bi, i: (bi, i, 0))
    out_r = pl.BlockSpec((1, t, LRU_W), lambda bi, i: (bi, n_tiles - 1 - i, 0))
    return pl.pallas_call(
        functools.partial(_lru_kernel, n_tiles=n_tiles),
        name="rglru",
        out_shape=[jax.ShapeDtypeStruct((b, s, LRU_W), F32)] * 2,
        grid=(b, n_tiles),
        in_specs=_halo_specs(LRU_W, t, n_tiles, s, False) + _halo_specs(LRU_W, t, n_tiles, s, True)
        + [full(conv_w), full(cb), full(wg), full(bg), full(lam)],
        out_specs=[out_f, out_r],
        scratch_shapes=[pltpu.VMEM((SUBLANES_V7X, LRU_W), F32)] * 2,
        compiler_params=_cparams(("parallel", "arbitrary")),
    )(xb, xb, xb, xb, xb, xb, conv_w, cb, wg, bg, lam)


def _lru_gate_weights(w_a, b_a, w_x, b_x, lam):
    def bd(w):
        eye = jnp.eye(LRU_BLOCKS, dtype=w.dtype)
        return jnp.einsum("dnij,nm->dnimj", w, eye).reshape(2, LRU_W, LRU_W)
    wg = jnp.concatenate([bd(w_a), bd(w_x)], axis=-1).astype(BF16)
    bg = jnp.concatenate([b_a, b_x], axis=-1).reshape(2, 1, 2 * LRU_W)
    return wg, bg, lam.reshape(2, 1, LRU_W)


def _rwkv_prep_kernel(z_ref, zp_ref, zn_ref, mu_ref, ones_ref, wup_ref, aup_ref, gup_ref, w0_ref, a0_ref,
                      kk_w_ref, ka_ref, rk_ref,
                      r_ref, v_ref, kk_ref, g_ref, bonus_ref, lw0_ref, kd0_ref, b0_ref, lw1_ref, kd1_ref, b1_ref,
                      *, n_tiles):
    i = pl.program_id(1)
    z = z_ref[0]
    t = z.shape[0]
    row = lax.broadcasted_iota(jnp.int32, z.shape, 0)
    pv = jnp.where(i > 0, zp_ref[0], 0.0)
    nx = jnp.where(i < n_tiles - 1, zn_ref[0], 0.0)
    prev = jnp.where(row == 0, pv[7:8], _shift_rows(z, 1))
    nxt = jnp.where(row == t - 1, nx[0:1], _shift_rows(z, -1))
    f = z + mu_ref[...] * (0.5 * (prev + nxt) - z)

    w = RWKV_W
    r = f[:, 0:w]
    k = f[:, w:2 * w]
    v = f[:, 2 * w:3 * w]
    lora = f[:, 3 * w:3 * w + W_LORA + A_LORA]
    xg = f[:, 3 * w + W_LORA + A_LORA:]
    ones = ones_ref[...]

    g_ref[0] = _dot(_sigmoid(xg), gup_ref[...])
    kk = k * kk_w_ref[...]
    ssq = _dot_split(kk * kk, ones, 2)
    kk = kk * lax.rsqrt(jnp.maximum(ssq, 1e-24))
    a_lora = _dot(lora, aup_ref[...])
    w_lora = jnp.tanh(lora)
    r_ref[0] = r
    v_ref[0] = v
    kk_ref[0] = kk
    bonus_ref[0] = _dot_split(r * k * rk_ref[...], ones, 2) * v
    ka = ka_ref[...]
    for d, (lw_ref, kd_ref, b_ref) in enumerate(((lw0_ref, kd0_ref, b0_ref), (lw1_ref, kd1_ref, b1_ref))):
        u = w0_ref[d] + _dot(w_lora, wup_ref[d])
        lw_ref[0] = -jnp.exp(-_softplus(-u) - 0.5)
        a = _sigmoid(a0_ref[d] + a_lora)
        kd_ref[0] = k * (1.0 + (a - 1.0) * ka)
        b_ref[0] = kk * a


def _rwkv_prep(zr, mu, w_up, w0, a_up, a0, g_up, k_k, k_a, r_k):
    b, s, _ = zr.shape
    t = min(SEQ_TILE, s)
    n_tiles = s // t
    full = lambda a: pl.BlockSpec(a.shape, lambda bi, i: (0,) * a.ndim)
    zeros = jnp.zeros((2, W_LORA, RWKV_W), F32)
    wup = jnp.concatenate([w_up, zeros], axis=1).astype(BF16)
    aup = jnp.concatenate([jnp.zeros((W_LORA, RWKV_W), F32), a_up], axis=0).astype(BF16)
    consts = [mu.reshape(1, RWKV_IN), _block_ones(RWKV_W, RWKV_HD), wup, aup, g_up.astype(BF16),
              w0.reshape(2, 1, RWKV_W), a0.reshape(2, 1, RWKV_W), k_k.reshape(1, RWKV_W),
              k_a.reshape(1, RWKV_W), r_k.reshape(1, RWKV_W)]
    out = pl.BlockSpec((1, t, RWKV_W), lambda bi, i: (bi, i, 0))
    return pl.pallas_call(
        functools.partial(_rwkv_prep_kernel, n_tiles=n_tiles),
        name="rwkv_prep",
        out_shape=[jax.ShapeDtypeStruct((b, s, RWKV_W), F32)] * 11,
        grid=(b, n_tiles),
        in_specs=_halo_specs(RWKV_IN, t, n_tiles, s, False) + [full(c) for c in consts],
        out_specs=[out] * 11,
        compiler_params=_cparams(("parallel", "parallel")),
    )(zr, zr, zr, *consts)


def _rwkv_masks(rev, c):
    hc = RWKV_HEADS * c
    rows = lax.broadcasted_iota(jnp.int32, (hc, hc), 0)
    cols = lax.broadcasted_iota(jnp.int32, (hc, hc), 1)
    lane = lax.broadcasted_iota(jnp.int32, (hc, RWKV_W), 1)
    srow = lax.broadcasted_iota(jnp.int32, (hc, RWKV_W), 0)
    ti = lax.broadcasted_iota(jnp.int32, (c, c), 0)
    tj = lax.broadcasted_iota(jnp.int32, (c, c), 1)
    if rev:
        strict, incl, tri, last = cols > rows, cols >= rows, tj >= ti, 0
    else:
        strict, incl, tri, last = cols < rows, cols <= rows, tj <= ti, c - 1
    return dict(strict=strict, incl=incl, tri=jnp.where(tri, 1.0, 0.0).astype(BF16), last=last,
                eye=rows == cols, head_mask=(srow // c) == (lane // RWKV_HD),
                same_block=lambda size: (rows // size) == (cols // size))


def _rwkv_tile(directions, chunk):
    c = chunk
    hc = RWKV_HEADS * c
    t = directions[0][0][0].shape[1]
    n_chunks = t // c
    jobs = []
    for step in range(n_chunks):
        for refs, y_ref, st_ref, rev in directions:
            ci = n_chunks - 1 - step if rev else step
            jobs.append(dict(refs=refs, y_ref=y_ref, st_ref=st_ref, sl=pl.ds(ci * c, c), m=_rwkv_masks(rev, c)))

    for j in jobs:
        m = j["m"]

        def stack(x, m=m):
            return jnp.where(m["head_mask"], jnp.concatenate([x] * RWKV_HEADS, axis=0), 0.0)

        r, v, kk, lw, kd, b = (ref[0, j["sl"], :] for ref in j["refs"])
        cum = _tri_cumsum(m["tri"], lw)
        inv_decay = jnp.exp(-cum)
        cum_last = cum[m["last"]:m["last"] + 1]
        to_end = jnp.exp(cum_last - cum)
        j["decay_end"] = jnp.exp(cum_last)
        j["a_t"] = stack(-kk * jnp.exp(cum - lw)).astype(BF16)
        j["r_t"] = stack(r * jnp.exp(cum)).astype(BF16)
        j["bk"] = jnp.concatenate([stack(b * inv_decay), stack(kd * inv_decay)], axis=0).astype(BF16)
        j["v_s"] = stack(v).astype(BF16)
        j["b_end"] = stack(b * to_end).astype(BF16)
        j["k_end"] = stack(kd * to_end).astype(BF16)

    for j in jobs:
        m = j["m"]
        g1 = _dot_nt(j["a_t"], j["bk"])
        j["a_ab"] = jnp.where(m["strict"], g1[:, :hc], 0.0)
        j["a_ak"] = jnp.where(m["strict"], g1[:, hc:], 0.0).astype(BF16)
        g2 = _dot_nt(j["r_t"], j.pop("bk"))
        j["a_rb"] = jnp.where(m["incl"], g2[:, :hc], 0.0).astype(BF16)
        j["a_rk"] = jnp.where(m["incl"], g2[:, hc:], 0.0).astype(BF16)
        j["tinv"] = jnp.where(m["eye"], 1.0, jnp.where(m["same_block"](2), j["a_ab"], 0.0))

    size = 2
    while size < c:
        for j in jobs:
            m = j["m"]
            off = jnp.where(m["same_block"](2 * size) & jnp.logical_not(m["same_block"](size)), j["a_ab"], 0.0)
            tinv = j["tinv"].astype(BF16)
            j["tinv"] = j["tinv"] + jnp.dot(tinv, _dot(off, tinv).astype(BF16), preferred_element_type=F32)
        size *= 2

    for j in jobs:
        j.pop("a_ab")
        w1 = _dot(j.pop("a_ak"), j["v_s"])
        j["x"] = _dot(j.pop("tinv"), jnp.concatenate([j.pop("a_t"), w1.astype(BF16)], axis=1)).astype(BF16)

    for j in jobs:
        x = j.pop("x")
        z = _dot(j.pop("a_rb"), x)
        j["r_bar"] = (j.pop("r_t").astype(F32) + z[:, :RWKV_W]).astype(BF16)
        j["y0"] = z[:, RWKV_W:] + _dot(j.pop("a_rk"), j["v_s"])
        gh = _dot_tn(j.pop("b_end"), x)
        j["trans"] = (jnp.where(j["m"]["eye"], j.pop("decay_end"), 0.0) + gh[:, :RWKV_W]).astype(BF16)
        j["h0"] = gh[:, RWKV_W:] + _dot_tn(j.pop("k_end"), j.pop("v_s"))

    for j in jobs:
        st = j["st_ref"][...].astype(BF16)
        y_s = jnp.dot(j["r_bar"], st, preferred_element_type=F32) + j["y0"]
        j["st_ref"][...] = jnp.dot(j["trans"], st, preferred_element_type=F32) + j["h0"]
        y = y_s[0:c]
        for hh in range(1, RWKV_HEADS):
            y = y + y_s[hh * c:(hh + 1) * c]
        j["y_ref"][0, j["sl"], :] = y


def _tri_cumsum(tri, x):
    acc = None
    rem = x
    for _ in range(3):
        hi = rem.astype(BF16)
        part = jnp.dot(tri, hi, preferred_element_type=F32)
        acc = part if acc is None else acc + part
        rem = rem - hi.astype(F32)
    return acc


def _rwkv_scan_kernel(rf_ref, vf_ref, kkf_ref, lwf_ref, kdf_ref, bf_ref, rr_ref, vr_ref, kkr_ref, lwr_ref, kdr_ref,
                      br_ref, yf_ref, yr_ref, sf_ref, sr_ref, *, chunk):
    @pl.when(pl.program_id(1) == 0)
    def _():
        sf_ref[...] = jnp.zeros_like(sf_ref)
        sr_ref[...] = jnp.zeros_like(sr_ref)

    _rwkv_tile([((rf_ref, vf_ref, kkf_ref, lwf_ref, kdf_ref, bf_ref), yf_ref, sf_ref, False),
                ((rr_ref, vr_ref, kkr_ref, lwr_ref, kdr_ref, br_ref), yr_ref, sr_ref, True)], chunk)


def _rwkv_scan(r, v, kk, lw0, kd0, b0, lw1, kd1, b1):
    b, s, _ = r.shape
    t = min(SEQ_TILE, s)
    chunk = min(RWKV_CHUNK, t)
    n_tiles = s // t
    fwd = pl.BlockSpec((1, t, RWKV_W), lambda bi, i: (bi, i, 0))
    bwd = pl.BlockSpec((1, t, RWKV_W), lambda bi, i: (bi, n_tiles - 1 - i, 0))
    return pl.pallas_call(
        functools.partial(_rwkv_scan_kernel, chunk=chunk),
        name="rwkv_scan",
        out_shape=[jax.ShapeDtypeStruct((b, s, RWKV_W), F32)] * 2,
        grid=(b, n_tiles),
        in_specs=[fwd] * 6 + [bwd] * 6,
        out_specs=[fwd, bwd],
        scratch_shapes=[pltpu.VMEM((RWKV_W, RWKV_W), F32)] * 2,
        compiler_params=_cparams(("parallel", "arbitrary")),
    )(r, v, kk, lw0, kd0, b0, r, v, kk, lw1, kd1, b1)


def _mixout_kernel(x_ref, mod_ref, hf_ref, hr_ref, yb_ref, yf_ref, yr_ref, g_ref, bonus_ref, oa_ref,
                   wl_ref, wr_ref, wa_ref, ones_ref, lng_ref, lnb_ref, o_ref):
    o_lru = (hf_ref[...] + hr_ref[...]) * _gelu_tanh(yb_ref[...])
    y = yf_ref[...] + yr_ref[...]
    ones = ones_ref[...]
    mu = _dot_split(y, ones, 3) * (1.0 / RWKV_HD)
    dlt = y - mu
    var = _dot_split(dlt * dlt, ones, 2) * (1.0 / RWKV_HD)
    yn = dlt * lax.rsqrt(var + GN_EPS) * lng_ref[...] + lnb_ref[...]
    o_rwkv = (yn + bonus_ref[...]) * g_ref[...]
    acc = _dot(o_lru, wl_ref[...]) + _dot(o_rwkv, wr_ref[...])
    for hh in range(ATT_HEADS):
        acc = acc + jnp.dot(oa_ref[0, hh], wa_ref[hh], preferred_element_type=F32)
    o_ref[...] = x_ref[...] + mod_ref[0, 5:6, :] * acc


def _mixout(x, mod, hf, hr, yb, yf, yr, g, bonus, oa, w_out, ln_g, ln_b):
    b, s, _ = x.shape
    n = b * s
    tm = min(MIX_TOKENS, s)
    nt = s // tm
    tok = lambda w: pl.BlockSpec((tm, w), lambda i: (i, 0))
    full = lambda a: pl.BlockSpec(a.shape, lambda i: (0,) * a.ndim)
    wl = w_out[:LRU_W]
    wr = w_out[LRU_W:LRU_W + RWKV_W]
    wa = w_out[LRU_W + RWKV_W:].reshape(ATT_HEADS, ATT_HD, D_MODEL)
    ones = _block_ones(RWKV_W, RWKV_HD)
    lng = ln_g.reshape(1, RWKV_W)
    lnb = ln_b.reshape(1, RWKV_W)
    flat = lambda a: a.reshape(n, a.shape[-1])
    out = pl.pallas_call(
        _mixout_kernel,
        name="mix_out",
        out_shape=jax.ShapeDtypeStruct((n, D_MODEL), F32),
        grid=(n // tm,),
        in_specs=[
            tok(D_MODEL),
            pl.BlockSpec((1, N_MOD, D_MODEL), lambda i: ((i * tm) // s, 0, 0)),
            tok(LRU_W), tok(LRU_W), tok(LRU_W), tok(RWKV_W), tok(RWKV_W), tok(RWKV_W), tok(RWKV_W),
            pl.BlockSpec((1, ATT_HEADS, tm, ATT_HD), lambda i: (i // nt, 0, i % nt, 0)),
            full(wl), full(wr), full(wa), full(ones), full(lng), full(lnb),
        ],
        out_specs=tok(D_MODEL),
        compiler_params=_cparams(("parallel",)),
    )(flat(x), mod, flat(hf), flat(hr), flat(yb), flat(yf), flat(yr), flat(g), flat(bonus), oa,
      wl, wr, wa, ones, lng, lnb)
    return out.reshape(b, s, D_MODEL)


def _layer(x, mod, lp, rope):
    x = _ffn(x, mod, 0, lp["norm_g"][0], lp["ffn_w_in"][0], lp["ffn_w_out"][0])
    xb, yb, zr, q, k, v = _mixin(x, mod, lp["norm_g"][1], lp["w_mix_in"], lp["attn_q_norm"], lp["attn_k_norm"], rope)
    hf, hr = _lru(xb, lp["lru_conv_w"], lp["lru_conv_b"], *lp["lru_gates"])
    r, vv, kk, g, bonus, lw0, kd0, b0, lw1, kd1, b1 = _rwkv_prep(
        zr, lp["rwkv_mu"], lp["rwkv_w_up"], lp["rwkv_w0"], lp["rwkv_a_up"], lp["rwkv_a0"], lp["rwkv_g_up"],
        lp["rwkv_k_k"], lp["rwkv_k_a"], lp["rwkv_r_k"])
    yf, yr = _rwkv_scan(r, vv, kk, lw0, kd0, b0, lw1, kd1, b1)
    oa = _attention(q, k, v)
    x = _mixout(x, mod, hf, hr, yb, yf, yr, g, bonus, oa, lp["w_mix_out"], lp["rwkv_ln_g"], lp["rwkv_ln_b"])
    x = _ffn(x, mod, 2, lp["norm_g"][2], lp["ffn_w_in"][1], lp["ffn_w_out"][1])
    return x


def kernel(x_prompt, x_sample, c_prompt, c_sample, w_ada, b_ada, norm_g, ffn_w_in, ffn_w_out, w_mix_in, w_mix_out, lru_conv_w, lru_conv_b, lru_w_gate_a, lru_b_gate_a, lru_w_gate_x, lru_b_gate_x, lru_lambda, rwkv_mu, rwkv_w_up, rwkv_w0, rwkv_a_up, rwkv_a0, rwkv_g_up, rwkv_k_k, rwkv_k_a, rwkv_r_k, rwkv_ln_g, rwkv_ln_b, attn_q_norm, attn_k_norm):
    depth = w_ada.shape[0]
    bp, bs = x_prompt.shape[0], x_sample.shape[0]
    rows = -(-(bp + bs) // SUBLANES_V7X) * SUBLANES_V7X
    c_all = jnp.concatenate([c_prompt, c_sample, jnp.zeros((rows - bp - bs, D_MODEL), F32)], axis=0)
    mod_all = _ada_mod(c_all, w_ada, b_ada)

    params = {
        "norm_g": norm_g, "ffn_w_in": ffn_w_in.astype(BF16), "ffn_w_out": ffn_w_out.astype(BF16),
        "w_mix_in": w_mix_in.astype(BF16), "w_mix_out": w_mix_out.astype(BF16),
        "lru_conv_w": lru_conv_w, "lru_conv_b": lru_conv_b, "rwkv_mu": rwkv_mu, "rwkv_w_up": rwkv_w_up,
        "rwkv_w0": rwkv_w0, "rwkv_a_up": rwkv_a_up, "rwkv_a0": rwkv_a0, "rwkv_g_up": rwkv_g_up,
        "rwkv_k_k": rwkv_k_k, "rwkv_k_a": rwkv_k_a, "rwkv_r_k": rwkv_r_k, "rwkv_ln_g": rwkv_ln_g,
        "rwkv_ln_b": rwkv_ln_b, "attn_q_norm": attn_q_norm, "attn_k_norm": attn_k_norm,
    }
    outs = []
    for x, lo, nb in ((x_prompt, 0, bp), (x_sample, bp, bs)):
        rope = _rope_tables(x.shape[1])
        for l in range(depth):
            lp = {name: arr[l] for name, arr in params.items()}
            lp["lru_gates"] = _lru_gate_weights(lru_w_gate_a[l], lru_b_gate_a[l], lru_w_gate_x[l],
                                                lru_b_gate_x[l], lru_lambda[l])
            mod = mod_all[l, lo:lo + nb].reshape(nb, N_MOD, D_MODEL)
            x = _layer(x, mod, lp, rope)
        outs.append(x)
    return tuple(outs)
```

```python
import functools
import math

import jax
import jax.numpy as jnp
from jax import lax
from jax.experimental import pallas as pl
from jax.experimental.pallas import tpu as pltpu

F32 = jnp.float32
BF16 = jnp.bfloat16

D_MODEL = 1024
D_FF = 2816
N_MOD = 9
GRID_W = 64
LRU_W = 384
LRU_BLOCKS = 6
LRU_BS = LRU_W // LRU_BLOCKS
LRU_C = 8.0
RWKV_HEADS = 4
RWKV_HD = 64
RWKV_W = RWKV_HEADS * RWKV_HD
W_LORA = 64
A_LORA = 64
G_LORA = 128
RWKV_IN = 3 * RWKV_W + W_LORA + A_LORA + G_LORA
ATT_HEADS = 6
ATT_KV = 2
ATT_G = ATT_HEADS // ATT_KV
ATT_HD = 64
ATT_Q = ATT_HEADS * ATT_HD
ATT_KVW = ATT_KV * ATT_HD
ROPE_THETA = 10000.0
ROPE_PAIRS = ATT_HD // 4
D_MIX = LRU_W + RWKV_W + ATT_Q
D_IN = 2 * LRU_W + RWKV_IN + ATT_Q + 2 * ATT_KVW
NORM_EPS = 1e-6
GN_EPS = 64e-5

LANES_V7X = 128
SUBLANES_V7X = 8
VMEM_LIMIT_BYTES = 56 * 1024 * 1024

FFN_TOKENS = 512
FFN_COLS = 256
MIX_TOKENS = 512
SEQ_TILE = 256
SCAN_TILE = 256
RWKV_CHUNK = 64
ATT_Q_TILE = 256
ATT_KV_TILE = 2048
ATT_KV_SUB = 256
ATT_VT_ROWS = 80
LOG2_E = 1.4426950408889634


def _cparams(sem):
    return pltpu.CompilerParams(dimension_semantics=sem, vmem_limit_bytes=VMEM_LIMIT_BYTES)


def _dot(a, b):
    return jnp.dot(a.astype(BF16), b.astype(BF16), preferred_element_type=F32)


def _dot_nt(a, b):
    return lax.dot_general(a.astype(BF16), b.astype(BF16), (((1,), (1,)), ((), ())),
                           preferred_element_type=F32)


def _dot_tn(a, b):
    return lax.dot_general(a.astype(BF16), b.astype(BF16), (((0,), (0,)), ((), ())),
                           preferred_element_type=F32)


def _dot_split(x, w, terms):
    acc = None
    rem = x
    for _ in range(terms):
        hi = rem.astype(BF16)
        part = jnp.dot(hi, w, preferred_element_type=F32)
        acc = part if acc is None else acc + part
        rem = rem - hi.astype(F32)
    return acc


def _sigmoid(x):
    return 1.0 / (1.0 + jnp.exp(-x))


def _silu(x):
    return x * _sigmoid(x)


def _softplus(x):
    return jnp.maximum(x, 0.0) + jnp.log(1.0 + jnp.exp(-jnp.abs(x)))


def _gelu_tanh(x):
    c = math.sqrt(2.0 / math.pi)
    return 0.5 * x * (1.0 + jnp.tanh(c * (x + 0.044715 * (x * x * x))))


def _rms_mod(x, g, scale, shift):
    ms = jnp.mean(x * x, axis=-1, keepdims=True)
    y = x * lax.rsqrt(ms + NORM_EPS) * g
    return y * (1.0 + scale) + shift


def _ada_kernel(c_ref, w_ref, b_ref, o_ref):
    c = c_ref[...]
    o_ref[0] = _dot(_silu(c), w_ref[0]) + b_ref[0]


def _ada_mod(c_all, w_ada, b_ada):
    depth = w_ada.shape[0]
    rows = c_all.shape[0]
    ncol = N_MOD * D_MODEL
    tn = 1152
    return pl.pallas_call(
        _ada_kernel,
        name="ada_mod",
        out_shape=jax.ShapeDtypeStruct((depth, rows, ncol), F32),
        grid=(depth, ncol // tn),
        in_specs=[
            pl.BlockSpec((rows, D_MODEL), lambda l, j: (0, 0)),
            pl.BlockSpec((1, D_MODEL, tn), lambda l, j: (l, 0, j)),
            pl.BlockSpec((1, 1, tn), lambda l, j: (l, 0, j)),
        ],
        out_specs=pl.BlockSpec((1, rows, tn), lambda l, j: (l, 0, j)),
        compiler_params=_cparams(("parallel", "parallel")),
    )(c_all, w_ada, b_ada.reshape(depth, 1, ncol))


def _ffn_kernel(x_ref, mod_ref, g_ref, wi_ref, wo_ref, o_ref, *, sub, tf):
    x = x_ref[...]
    h = _rms_mod(x, g_ref[...], mod_ref[0, 3 * sub + 1:3 * sub + 2, :], mod_ref[0, 3 * sub:3 * sub + 1, :])
    h = h.astype(BF16)
    n_chunks = D_FF // tf

    def gate_up(c):
        return (jnp.dot(h, wi_ref[:, c * tf:(c + 1) * tf], preferred_element_type=F32),
                jnp.dot(h, wi_ref[:, D_FF + c * tf:D_FF + (c + 1) * tf], preferred_element_type=F32))

    pending = gate_up(0)
    acc = None
    for c in range(n_chunks):
        gate, up = pending
        if c + 1 < n_chunks:
            pending = gate_up(c + 1)
        act = (_silu(gate) * up).astype(BF16)
        part = jnp.dot(act, wo_ref[c * tf:(c + 1) * tf, :], preferred_element_type=F32)
        acc = part if acc is None else acc + part
    o_ref[...] = x + 0.5 * mod_ref[0, 3 * sub + 2:3 * sub + 3, :] * acc


def _resident(a):
    return pl.BlockSpec(a.shape, lambda *_: (0,) * a.ndim, pipeline_mode=pl.Buffered(1))


def _ffn(x, mod, sub, g, w_in, w_out):
    b, s, _ = x.shape
    n = b * s
    tm = min(FFN_TOKENS, s)
    g2 = g.reshape(1, D_MODEL)
    out = pl.pallas_call(
        functools.partial(_ffn_kernel, sub=sub, tf=FFN_COLS),
        name="ffn",
        out_shape=jax.ShapeDtypeStruct((n, D_MODEL), F32),
        grid=(n // tm,),
        in_specs=[
            pl.BlockSpec((tm, D_MODEL), lambda i: (i, 0)),
            pl.BlockSpec((1, N_MOD, D_MODEL), lambda i: ((i * tm) // s, 0, 0)),
            _resident(g2), _resident(w_in), _resident(w_out),
        ],
        out_specs=pl.BlockSpec((tm, D_MODEL), lambda i: (i, 0)),
        compiler_params=_cparams(("parallel",)),
    )(x.reshape(n, D_MODEL), mod, g2, w_in, w_out)
    return out.reshape(b, s, D_MODEL)


def _rope_rotate(x, cos, sin_signed):
    outs = []
    for c in range(x.shape[1] // LANES_V7X):
        sl = slice(c * LANES_V7X, (c + 1) * LANES_V7X)
        xc = x[:, sl]
        lane = lax.broadcasted_iota(jnp.int32, xc.shape, 1)
        first = (lane % 32) < 16
        partner = jnp.where(first, pltpu.roll(xc, LANES_V7X - 16, 1), pltpu.roll(xc, 16, 1))
        outs.append(xc * cos[:, sl] + partner * sin_signed[:, sl])
    return jnp.concatenate(outs, axis=1) if len(outs) > 1 else outs[0]


def _mixin_kernel(x_ref, mod_ref, g_ref, w_ref, onesq_ref, gq_ref, gk_ref, cq_ref, sq_ref, ck_ref, sk_ref,
                  xb_ref, yb_ref, zr_ref, q_ref, k_ref, v_ref):
    h = _rms_mod(x_ref[...], g_ref[...], mod_ref[0, 4:5, :], mod_ref[0, 3:4, :]).astype(BF16)

    def proj(lo, hi):
        return jnp.dot(h, w_ref[:, lo:hi], preferred_element_type=F32)

    o1 = 2 * LRU_W + RWKV_IN
    q = proj(o1, o1 + ATT_Q)
    k = proj(o1 + ATT_Q, o1 + ATT_Q + ATT_KVW)
    v = proj(o1 + ATT_Q + ATT_KVW, D_IN)
    ones = onesq_ref[...]

    zr_ref[...] = proj(2 * LRU_W, o1)
    ssq_q = _dot_split(q * q, ones, 2)
    qn = q * lax.rsqrt(ssq_q * (1.0 / ATT_HD) + NORM_EPS) * gq_ref[...]
    qr = _rope_rotate(qn, cq_ref[...], sq_ref[...]) * (ATT_HD ** -0.5 * LOG2_E)
    for hh in range(ATT_HEADS):
        q_ref[0, hh] = qr[:, hh * ATT_HD:(hh + 1) * ATT_HD].astype(BF16)

    xb_ref[...] = proj(0, LRU_W)
    ssq_k = _dot_split(k * k, ones[:ATT_KVW, :ATT_KVW], 2)
    kn = k * lax.rsqrt(ssq_k * (1.0 / ATT_HD) + NORM_EPS) * gk_ref[...]
    kr = _rope_rotate(kn, ck_ref[...], sk_ref[...])
    for hh in range(ATT_KV):
        k_ref[0, hh] = kr[:, hh * ATT_HD:(hh + 1) * ATT_HD].astype(BF16)
        v_ref[0, hh] = v[:, hh * ATT_HD:(hh + 1) * ATT_HD].astype(BF16)

    yb_ref[...] = proj(LRU_W, 2 * LRU_W)


def _rope_tables(seq):
    n_rows = seq // GRID_W
    row = jnp.repeat(jnp.arange(n_rows, dtype=F32), GRID_W)
    col = jnp.tile(jnp.arange(GRID_W, dtype=F32), n_rows)
    inv = ROPE_THETA ** (-jnp.arange(ROPE_PAIRS, dtype=F32) / ROPE_PAIRS)
    ang_r = row[:, None] * inv
    ang_c = col[:, None] * inv
    cos_h = jnp.concatenate([jnp.cos(ang_r), jnp.cos(ang_r), jnp.cos(ang_c), jnp.cos(ang_c)], axis=1)
    sin_h = jnp.concatenate([-jnp.sin(ang_r), jnp.sin(ang_r), -jnp.sin(ang_c), jnp.sin(ang_c)], axis=1)
    return (jnp.tile(cos_h, (1, ATT_HEADS)), jnp.tile(sin_h, (1, ATT_HEADS)),
            jnp.tile(cos_h, (1, ATT_KV)), jnp.tile(sin_h, (1, ATT_KV)))


def _block_ones(width, block):
    idx = jnp.arange(width)
    return (idx[:, None] // block == idx[None, :] // block).astype(BF16)


def _mixin(x, mod, g, w_in, gq, gk, rope):
    b, s, _ = x.shape
    n = b * s
    tm = min(MIX_TOKENS, s)
    nt = s // tm
    cq, sq, ck, sk = rope
    tok = lambda w: pl.BlockSpec((tm, w), lambda i: (i, 0))
    pos = lambda w: pl.BlockSpec((tm, w), lambda i: (i % nt, 0))
    full = lambda a: pl.BlockSpec(a.shape, lambda i: (0,) * a.ndim)
    ones = _block_ones(ATT_Q, ATT_HD)
    gq_t = jnp.tile(gq.reshape(1, ATT_HD), (1, ATT_HEADS))
    gk_t = jnp.tile(gk.reshape(1, ATT_HD), (1, ATT_KV))
    g2 = g.reshape(1, D_MODEL)
    head = lambda nh: pl.BlockSpec((1, nh, tm, ATT_HD), lambda i: (i // nt, 0, i % nt, 0))
    outs = pl.pallas_call(
        _mixin_kernel,
        name="mix_in",
        out_shape=[
            jax.ShapeDtypeStruct((n, LRU_W), F32),
            jax.ShapeDtypeStruct((n, LRU_W), F32),
            jax.ShapeDtypeStruct((n, RWKV_IN), F32),
            jax.ShapeDtypeStruct((b, ATT_HEADS, s, ATT_HD), BF16),
            jax.ShapeDtypeStruct((b, ATT_KV, s, ATT_HD), BF16),
            jax.ShapeDtypeStruct((b, ATT_KV, s, ATT_HD), BF16),
        ],
        grid=(n // tm,),
        in_specs=[
            tok(D_MODEL),
            pl.BlockSpec((1, N_MOD, D_MODEL), lambda i: ((i * tm) // s, 0, 0)),
            full(g2), full(w_in), full(ones), full(gq_t), full(gk_t),
            pos(ATT_Q), pos(ATT_Q), pos(ATT_KVW), pos(ATT_KVW),
        ],
        out_specs=[tok(LRU_W), tok(LRU_W), tok(RWKV_IN), head(ATT_HEADS), head(ATT_KV), head(ATT_KV)],
        compiler_params=_cparams(("parallel",)),
    )(x.reshape(n, D_MODEL), mod, g2, w_in, ones, gq_t, gk_t, cq, sq, ck, sk)
    xb, yb, zr, q, k, v = outs
    return (xb.reshape(b, s, LRU_W), yb.reshape(b, s, LRU_W), zr.reshape(b, s, RWKV_IN), q, k, v)


def _attn_kernel(q_ref, k_ref, vt_ref, o_ref, m_ref, acc_ref, *, tq, nk):
    ki = pl.program_id(2)

    @pl.when(ki == 0)
    def _():
        m_ref[...] = jnp.full_like(m_ref, -jnp.inf)
        acc_ref[...] = jnp.zeros_like(acc_ref)

    tk = k_ref.shape[2]
    sub = min(ATT_KV_SUB, tk)
    qs = [q_ref[0, ATT_G * g:ATT_G * (g + 1)].reshape(ATT_G * tq, ATT_HD) for g in range(ATT_KV)]
    jobs = [(g, c * sub) for c in range(tk // sub) for g in range(ATT_KV)]

    def scores(job):
        g, lo = job
        return lax.dot_general(k_ref[0, g, lo:lo + sub, :], qs[g], (((1,), (1,)), ((), ())),
                               preferred_element_type=F32)

    ahead = 2
    pending = {i: scores(jobs[i]) for i in range(min(ahead, len(jobs)))}
    for i, (g, lo) in enumerate(jobs):
        if i + ahead < len(jobs):
            pending[i + ahead] = scores(jobs[i + ahead])
        st = pending.pop(i)
        m_prev = m_ref[g]
        m_new = jnp.maximum(m_prev, jnp.max(st, axis=0, keepdims=True))
        p = jnp.exp2(st - m_new).astype(BF16)
        acc_ref[g] = jnp.exp2(m_prev - m_new) * acc_ref[g] + jnp.dot(vt_ref[0, g, :, lo:lo + sub], p,
                                                                      preferred_element_type=F32)
        m_ref[g] = m_new

    @pl.when(ki == nk - 1)
    def _():
        heads = []
        for g in range(ATT_KV):
            acc = acc_ref[g]
            o = (acc[:ATT_HD] / acc[ATT_HD:ATT_HD + 1]).T
            heads += [o[hh * tq:(hh + 1) * tq] for hh in range(ATT_G)]
        o_ref[0] = jnp.concatenate(heads, axis=1).astype(BF16)


def _attention(q, k, v):
    b, _, s, _ = q.shape
    ones_rows = jnp.ones((b, ATT_KV, ATT_VT_ROWS - ATT_HD, s), BF16)
    vt = jnp.concatenate([jnp.swapaxes(v, 2, 3), ones_rows], axis=2)
    tq = min(ATT_Q_TILE, s)
    tk = min(ATT_KV_TILE, s)
    nk = s // tk
    return pl.pallas_call(
        functools.partial(_attn_kernel, tq=tq, nk=nk),
        name="attention",
        out_shape=jax.ShapeDtypeStruct((b, s, ATT_Q), BF16),
        grid=(b, s // tq, nk),
        in_specs=[
            pl.BlockSpec((1, ATT_HEADS, tq, ATT_HD), lambda bi, qi, ki: (bi, 0, qi, 0)),
            pl.BlockSpec((1, ATT_KV, tk, ATT_HD), lambda bi, qi, ki: (bi, 0, ki, 0)),
            pl.BlockSpec((1, ATT_KV, ATT_VT_ROWS, tk), lambda bi, qi, ki: (bi, 0, 0, ki)),
        ],
        out_specs=pl.BlockSpec((1, tq, ATT_Q), lambda bi, qi, ki: (bi, qi, 0)),
        scratch_shapes=[
            pltpu.VMEM((ATT_KV, 1, ATT_G * tq), F32),
            pltpu.VMEM((ATT_KV, ATT_VT_ROWS, ATT_G * tq), F32),
        ],
        compiler_params=_cparams(("parallel", "parallel", "arbitrary")),
    )(q, k, vt)


def _shift_rows(x, shift):
    return pltpu.roll(x, shift % x.shape[0], 0)


def _lru_direction(x_ref, xp_ref, xn_ref, cw_ref, cb_ref, wg_ref, bg_ref, lam_ref, h_ref, carry_ref, ext_ref,
                   tile, n_tiles, rev):
    x = x_ref[0]
    t = x.shape[0]
    grp = SUBLANES_V7X
    ext_ref[0:grp, :] = jnp.where(tile > 0, xp_ref[0], 0.0)
    ext_ref[grp:grp + t, :] = x
    ext_ref[grp + t:, :] = jnp.where(tile < n_tiles - 1, xn_ref[0], 0.0)
    cw = cw_ref[...]
    xc = (cw[0:1] * ext_ref[grp - 2:grp - 2 + t, :] + cw[1:2] * ext_ref[grp - 1:grp - 1 + t, :] + cw[2:3] * x
          + cw[3:4] * ext_ref[grp + 1:grp + 1 + t, :] + cb_ref[...])

    gates = _dot(xc, wg_ref[...]) + bg_ref[...]
    r = _sigmoid(gates[:, :LRU_W])
    gi = _sigmoid(gates[:, LRU_W:])
    log_a = (-LRU_C) * r * _softplus(-lam_ref[...])
    a = jnp.exp(log_a)
    u = jnp.sqrt(1.0 - a * a) * (gi * xc)

    a = a.reshape(t // grp, grp, LRU_W)
    u = u.reshape(t // grp, grp, LRU_W)
    pos = lax.broadcasted_iota(jnp.int32, a.shape, 1)
    step = 1
    while step < grp:
        if not rev:
            valid = pos >= step
            shift = step
        else:
            valid = pos < grp - step
            shift = grp - step
        a_sh = jnp.where(valid, pltpu.roll(a, shift, 1), 1.0)
        u_sh = jnp.where(valid, pltpu.roll(u, shift, 1), 0.0)
        u = u + a * u_sh
        a = a * a_sh
        step *= 2
    a = a.reshape(t, LRU_W)
    u = u.reshape(t, LRU_W)
    carry = carry_ref[0:1, :]
    groups = range(t // grp)
    for gi in (reversed(groups) if rev else groups):
        sl = slice(gi * grp, (gi + 1) * grp)
        h = u[sl] + a[sl] * carry
        h_ref[0, sl, :] = h
        carry = h[0:1] if rev else h[grp - 1:grp]
    carry_ref[...] = jnp.broadcast_to(carry, carry_ref.shape)


def _lru_kernel(xf_ref, xfp_ref, xfn_ref, xr_ref, xrp_ref, xrn_ref, cw_ref, cb_ref, wg_ref, bg_ref, lam_ref,
                hf_ref, hr_ref, cf_ref, cr_ref, ef_ref, er_ref, *, n_tiles):
    i = pl.program_id(1)

    @pl.when(i == 0)
    def _():
        cf_ref[...] = jnp.zeros_like(cf_ref)
        cr_ref[...] = jnp.zeros_like(cr_ref)

    _lru_direction(xf_ref, xfp_ref, xfn_ref, cw_ref, cb_ref, wg_ref.at[0], bg_ref.at[0], lam_ref.at[0],
                   hf_ref, cf_ref, ef_ref, i, n_tiles, False)
    _lru_direction(xr_ref, xrp_ref, xrn_ref, cw_ref, cb_ref, wg_ref.at[1], bg_ref.at[1], lam_ref.at[1],
                   hr_ref, cr_ref, er_ref, n_tiles - 1 - i, n_tiles, True)


def _halo_specs(width, t, n_tiles, s, rev):
    r8 = t // SUBLANES_V7X
    last8 = s // SUBLANES_V7X - 1
    tile = (lambda i: n_tiles - 1 - i) if rev else (lambda i: i)
    return [
        pl.BlockSpec((1, t, width), lambda b, i: (b, tile(i), 0)),
        pl.BlockSpec((1, SUBLANES_V7X, width), lambda b, i: (b, jnp.maximum(tile(i) * r8 - 1, 0), 0)),
        pl.BlockSpec((1, SUBLANES_V7X, width), lambda b, i: (b, jnp.minimum((tile(i) + 1) * r8, last8), 0)),
    ]


def _lru(xb, conv_w, conv_b, wg, bg, lam):
    b, s, _ = xb.shape
    t = min(SEQ_TILE, s)
    n_tiles = s // t
    full = lambda a: pl.BlockSpec(a.shape, lambda bi, i: (0,) * a.ndim)
    cb = conv_b.reshape(1, LRU_W)
    out_f = pl.BlockSpec((1, t, LRU_W), lambda bi, i: (bi, i, 0))
    out_r = pl.BlockSpec((1, t, LRU_W), lambda bi, i: (bi, n_tiles - 1 - i, 0))
    return pl.pallas_call(
        functools.partial(_lru_kernel, n_tiles=n_tiles),
        name="rglru",
        out_shape=[jax.ShapeDtypeStruct((b, s, LRU_W), F32)] * 2,
        grid=(b, n_tiles),
        in_specs=_halo_specs(LRU_W, t, n_tiles, s, False) + _halo_specs(LRU_W, t, n_tiles, s, True)
        + [full(conv_w), full(cb), full(wg), full(bg), full(lam)],
        out_specs=[out_f, out_r],
        scratch_shapes=[pltpu.VMEM((SUBLANES_V7X, LRU_W), F32)] * 2
        + [pltpu.VMEM((t + 2 * SUBLANES_V7X, LRU_W), F32)] * 2,
        compiler_params=_cparams(("parallel", "arbitrary")),
    )(xb, xb, xb, xb, xb, xb, conv_w, cb, wg, bg, lam)


def _lru_gate_weights(w_a, b_a, w_x, b_x, lam):
    def bd(w):
        eye = jnp.eye(LRU_BLOCKS, dtype=w.dtype)
        return jnp.einsum("dnij,nm->dnimj", w, eye).reshape(2, LRU_W, LRU_W)
    wg = jnp.concatenate([bd(w_a), bd(w_x)], axis=-1).astype(BF16)
    bg = jnp.concatenate([b_a, b_x], axis=-1).reshape(2, 1, 2 * LRU_W)
    return wg, bg, lam.reshape(2, 1, LRU_W)


def _rwkv_prep_kernel(z_ref, zp_ref, zn_ref, mu_ref, ones_ref, wup_ref, aup_ref, gup_ref, w0_ref, a0_ref,
                      kk_w_ref, ka_ref, rk_ref,
                      r_ref, v_ref, kk_ref, g_ref, bonus_ref, lw0_ref, kd0_ref, b0_ref, lw1_ref, kd1_ref, b1_ref,
                      *, n_tiles):
    i = pl.program_id(1)
    z = z_ref[0]
    t = z.shape[0]
    row = lax.broadcasted_iota(jnp.int32, z.shape, 0)
    pv = jnp.where(i > 0, zp_ref[0], 0.0)
    nx = jnp.where(i < n_tiles - 1, zn_ref[0], 0.0)
    prev = jnp.where(row == 0, pv[7:8], _shift_rows(z, 1))
    nxt = jnp.where(row == t - 1, nx[0:1], _shift_rows(z, -1))
    f = z + mu_ref[...] * (0.5 * (prev + nxt) - z)

    w = RWKV_W
    r = f[:, 0:w]
    k = f[:, w:2 * w]
    v = f[:, 2 * w:3 * w]
    lora = f[:, 3 * w:3 * w + W_LORA + A_LORA]
    xg = f[:, 3 * w + W_LORA + A_LORA:]
    ones = ones_ref[...]

    g_ref[0] = _dot(_sigmoid(xg), gup_ref[...])
    kk = k * kk_w_ref[...]
    ssq = _dot_split(kk * kk, ones, 2)
    kk = kk * lax.rsqrt(jnp.maximum(ssq, 1e-24))
    a_lora = _dot(lora, aup_ref[...])
    w_lora = jnp.tanh(lora)
    r_ref[0] = r
    v_ref[0] = v
    kk_ref[0] = kk
    bonus_ref[0] = _dot_split(r * k * rk_ref[...], ones, 2) * v
    ka = ka_ref[...]
    for d, (lw_ref, kd_ref, b_ref) in enumerate(((lw0_ref, kd0_ref, b0_ref), (lw1_ref, kd1_ref, b1_ref))):
        u = w0_ref[d] + _dot(w_lora, wup_ref[d])
        lw_ref[0] = -jnp.exp(-_softplus(-u) - 0.5)
        a = _sigmoid(a0_ref[d] + a_lora)
        kd_ref[0] = k * (1.0 + (a - 1.0) * ka)
        b_ref[0] = kk * a


def _rwkv_prep(zr, mu, w_up, w0, a_up, a0, g_up, k_k, k_a, r_k):
    b, s, _ = zr.shape
    t = min(SEQ_TILE, s)
    n_tiles = s // t
    full = lambda a: pl.BlockSpec(a.shape, lambda bi, i: (0,) * a.ndim)
    zeros = jnp.zeros((2, W_LORA, RWKV_W), F32)
    wup = jnp.concatenate([w_up, zeros], axis=1).astype(BF16)
    aup = jnp.concatenate([jnp.zeros((W_LORA, RWKV_W), F32), a_up], axis=0).astype(BF16)
    consts = [mu.reshape(1, RWKV_IN), _block_ones(RWKV_W, RWKV_HD), wup, aup, g_up.astype(BF16),
              w0.reshape(2, 1, RWKV_W), a0.reshape(2, 1, RWKV_W), k_k.reshape(1, RWKV_W),
              k_a.reshape(1, RWKV_W), r_k.reshape(1, RWKV_W)]
    out = pl.BlockSpec((1, t, RWKV_W), lambda bi, i: (bi, i, 0))
    return pl.pallas_call(
        functools.partial(_rwkv_prep_kernel, n_tiles=n_tiles),
        name="rwkv_prep",
        out_shape=[jax.ShapeDtypeStruct((b, s, RWKV_W), F32)] * 11,
        grid=(b, n_tiles),
        in_specs=_halo_specs(RWKV_IN, t, n_tiles, s, False) + [full(c) for c in consts],
        out_specs=[out] * 11,
        compiler_params=_cparams(("parallel", "parallel")),
    )(zr, zr, zr, *consts)


def _rwkv_masks(rev, c):
    hc = RWKV_HEADS * c
    rows = lax.broadcasted_iota(jnp.int32, (hc, hc), 0)
    cols = lax.broadcasted_iota(jnp.int32, (hc, hc), 1)
    lane = lax.broadcasted_iota(jnp.int32, (hc, RWKV_W), 1)
    srow = lax.broadcasted_iota(jnp.int32, (hc, RWKV_W), 0)
    ti = lax.broadcasted_iota(jnp.int32, (c, c), 0)
    tj = lax.broadcasted_iota(jnp.int32, (c, c), 1)
    if rev:
        strict, incl, tri, last = cols > rows, cols >= rows, tj >= ti, 0
    else:
        strict, incl, tri, last = cols < rows, cols <= rows, tj <= ti, c - 1
    return dict(strict=strict, incl=incl, tri=jnp.where(tri, 1.0, 0.0).astype(BF16), last=last,
                eye=rows == cols, head_mask=(srow // c) == (lane // RWKV_HD),
                same_block=lambda size: (rows // size) == (cols // size))


def _rwkv_tile(directions, chunk):
    c = chunk
    hc = RWKV_HEADS * c
    t = directions[0][0][0].shape[1]
    n_chunks = t // c
    jobs = []
    for step in range(n_chunks):
        for refs, y_ref, st_ref, rev in directions:
            ci = n_chunks - 1 - step if rev else step
            jobs.append(dict(refs=refs, y_ref=y_ref, st_ref=st_ref, sl=pl.ds(ci * c, c), m=_rwkv_masks(rev, c)))

    for j in jobs:
        m = j["m"]

        def stack(x, m=m):
            return jnp.where(m["head_mask"], jnp.concatenate([x] * RWKV_HEADS, axis=0), 0.0)

        r, v, kk, lw, kd, b = (ref[0, j["sl"], :] for ref in j["refs"])
        cum = _tri_cumsum(m["tri"], lw)
        inv_decay = jnp.exp(-cum)
        cum_last = cum[m["last"]:m["last"] + 1]
        to_end = jnp.exp(cum_last - cum)
        j["decay_end"] = jnp.exp(cum_last)
        j["a_t"] = stack(-kk * jnp.exp(cum - lw)).astype(BF16)
        j["r_t"] = stack(r * jnp.exp(cum)).astype(BF16)
        j["bk"] = jnp.concatenate([stack(b * inv_decay), stack(kd * inv_decay)], axis=0).astype(BF16)
        j["v_s"] = stack(v).astype(BF16)
        j["b_end"] = stack(b * to_end).astype(BF16)
        j["k_end"] = stack(kd * to_end).astype(BF16)

    for j in jobs:
        m = j["m"]
        g1 = _dot_nt(j["a_t"], j["bk"])
        j["a_ab"] = jnp.where(m["strict"], g1[:, :hc], 0.0)
        j["a_ak"] = jnp.where(m["strict"], g1[:, hc:], 0.0).astype(BF16)
        g2 = _dot_nt(j["r_t"], j.pop("bk"))
        j["a_rb"] = jnp.where(m["incl"], g2[:, :hc], 0.0).astype(BF16)
        j["a_rk"] = jnp.where(m["incl"], g2[:, hc:], 0.0).astype(BF16)
        j["tinv"] = jnp.where(m["eye"], 1.0, jnp.where(m["same_block"](2), j["a_ab"], 0.0))

    size = 2
    while size < c:
        for j in jobs:
            m = j["m"]
            off = jnp.where(m["same_block"](2 * size) & jnp.logical_not(m["same_block"](size)), j["a_ab"], 0.0)
            tinv = j["tinv"].astype(BF16)
            j["tinv"] = j["tinv"] + jnp.dot(tinv, _dot(off, tinv).astype(BF16), preferred_element_type=F32)
        size *= 2

    for j in jobs:
        j.pop("a_ab")
        w1 = _dot(j.pop("a_ak"), j["v_s"])
        j["x"] = _dot(j.pop("tinv"), jnp.concatenate([j.pop("a_t"), w1.astype(BF16)], axis=1)).astype(BF16)

    for j in jobs:
        x = j.pop("x")
        z = _dot(j.pop("a_rb"), x)
        j["r_bar"] = (j.pop("r_t").astype(F32) + z[:, :RWKV_W]).astype(BF16)
        j["y0"] = z[:, RWKV_W:] + _dot(j.pop("a_rk"), j["v_s"])
        gh = _dot_tn(j.pop("b_end"), x)
        j["trans"] = (jnp.where(j["m"]["eye"], j.pop("decay_end"), 0.0) + gh[:, :RWKV_W]).astype(BF16)
        j["h0"] = gh[:, RWKV_W:] + _dot_tn(j.pop("k_end"), j.pop("v_s"))

    for j in jobs:
        st = j["st_ref"][...].astype(BF16)
        y_s = jnp.dot(j["r_bar"], st, preferred_element_type=F32) + j["y0"]
        j["st_ref"][...] = jnp.dot(j["trans"], st, preferred_element_type=F32) + j["h0"]
        y = y_s[0:c]
        for hh in range(1, RWKV_HEADS):
            y = y + y_s[hh * c:(hh + 1) * c]
        j["y_ref"][0, j["sl"], :] = y


def _tri_cumsum(tri, x):
    acc = None
    rem = x
    for _ in range(3):
        hi = rem.astype(BF16)
        part = jnp.dot(tri, hi, preferred_element_type=F32)
        acc = part if acc is None else acc + part
        rem = rem - hi.astype(F32)
    return acc


def _rwkv_scan_kernel(rf_ref, vf_ref, kkf_ref, lwf_ref, kdf_ref, bf_ref, rr_ref, vr_ref, kkr_ref, lwr_ref, kdr_ref,
                      br_ref, yf_ref, yr_ref, sf_ref, sr_ref, *, chunk):
    @pl.when(pl.program_id(1) == 0)
    def _():
        sf_ref[...] = jnp.zeros_like(sf_ref)
        sr_ref[...] = jnp.zeros_like(sr_ref)

    _rwkv_tile([((rf_ref, vf_ref, kkf_ref, lwf_ref, kdf_ref, bf_ref), yf_ref, sf_ref, False),
                ((rr_ref, vr_ref, kkr_ref, lwr_ref, kdr_ref, br_ref), yr_ref, sr_ref, True)], chunk)


def _rwkv_scan(r, v, kk, lw0, kd0, b0, lw1, kd1, b1):
    b, s, _ = r.shape
    t = min(SCAN_TILE, s)
    chunk = min(RWKV_CHUNK, t)
    n_tiles = s // t
    fwd = pl.BlockSpec((1, t, RWKV_W), lambda bi, i: (bi, i, 0))
    bwd = pl.BlockSpec((1, t, RWKV_W), lambda bi, i: (bi, n_tiles - 1 - i, 0))
    return pl.pallas_call(
        functools.partial(_rwkv_scan_kernel, chunk=chunk),
        name="rwkv_scan",
        out_shape=[jax.ShapeDtypeStruct((b, s, RWKV_W), F32)] * 2,
        grid=(b, n_tiles),
        in_specs=[fwd] * 6 + [bwd] * 6,
        out_specs=[fwd, bwd],
        scratch_shapes=[pltpu.VMEM((RWKV_W, RWKV_W), F32)] * 2,
        compiler_params=_cparams(("parallel", "arbitrary")),
    )(r, v, kk, lw0, kd0, b0, r, v, kk, lw1, kd1, b1)


def _mixout_kernel(x_ref, mod_ref, hf_ref, hr_ref, yb_ref, yf_ref, yr_ref, g_ref, bonus_ref, oa_ref,
                   wl_ref, wr_ref, wa_ref, ones_ref, lng_ref, lnb_ref, o_ref):
    o_lru = (hf_ref[...] + hr_ref[...]) * _gelu_tanh(yb_ref[...])
    y = yf_ref[...] + yr_ref[...]
    ones = ones_ref[...]
    mu = _dot_split(y, ones, 3) * (1.0 / RWKV_HD)
    dlt = y - mu
    var = _dot_split(dlt * dlt, ones, 2) * (1.0 / RWKV_HD)
    yn = dlt * lax.rsqrt(var + GN_EPS) * lng_ref[...] + lnb_ref[...]
    o_rwkv = (yn + bonus_ref[...]) * g_ref[...]
    acc = (_dot(o_lru, wl_ref[...]) + _dot(o_rwkv, wr_ref[...])
           + jnp.dot(oa_ref[...], wa_ref[...], preferred_element_type=F32))
    o_ref[...] = x_ref[...] + mod_ref[0, 5:6, :] * acc


def _mixout(x, mod, hf, hr, yb, yf, yr, g, bonus, oa, w_out, ln_g, ln_b):
    b, s, _ = x.shape
    n = b * s
    tm = min(MIX_TOKENS, s)
    nt = s // tm
    tok = lambda w: pl.BlockSpec((tm, w), lambda i: (i, 0))
    full = lambda a: pl.BlockSpec(a.shape, lambda i: (0,) * a.ndim)
    wl = w_out[:LRU_W]
    wr = w_out[LRU_W:LRU_W + RWKV_W]
    wa = w_out[LRU_W + RWKV_W:]
    ones = _block_ones(RWKV_W, RWKV_HD)
    lng = ln_g.reshape(1, RWKV_W)
    lnb = ln_b.reshape(1, RWKV_W)
    flat = lambda a: a.reshape(n, a.shape[-1])
    out = pl.pallas_call(
        _mixout_kernel,
        name="mix_out",
        out_shape=jax.ShapeDtypeStruct((n, D_MODEL), F32),
        grid=(n // tm,),
        in_specs=[
            tok(D_MODEL),
            pl.BlockSpec((1, N_MOD, D_MODEL), lambda i: ((i * tm) // s, 0, 0)),
            tok(LRU_W), tok(LRU_W), tok(LRU_W), tok(RWKV_W), tok(RWKV_W), tok(RWKV_W), tok(RWKV_W),
            tok(ATT_Q),
            full(wl), full(wr), full(wa), full(ones), full(lng), full(lnb),
        ],
        out_specs=tok(D_MODEL),
        compiler_params=_cparams(("parallel",)),
    )(flat(x), mod, flat(hf), flat(hr), flat(yb), flat(yf), flat(yr), flat(g), flat(bonus), flat(oa),
      wl, wr, wa, ones, lng, lnb)
    return out.reshape(b, s, D_MODEL)


def _layer(x, mod, lp, rope):
    x = _ffn(x, mod, 0, lp["norm_g"][0], lp["ffn_w_in"][0], lp["ffn_w_out"][0])
    xb, yb, zr, q, k, v = _mixin(x, mod, lp["norm_g"][1], lp["w_mix_in"], lp["attn_q_norm"], lp["attn_k_norm"], rope)
    hf, hr = _lru(xb, lp["lru_conv_w"], lp["lru_conv_b"], *lp["lru_gates"])
    r, vv, kk, g, bonus, lw0, kd0, b0, lw1, kd1, b1 = _rwkv_prep(
        zr, lp["rwkv_mu"], lp["rwkv_w_up"], lp["rwkv_w0"], lp["rwkv_a_up"], lp["rwkv_a0"], lp["rwkv_g_up"],
        lp["rwkv_k_k"], lp["rwkv_k_a"], lp["rwkv_r_k"])
    yf, yr = _rwkv_scan(r, vv, kk, lw0, kd0, b0, lw1, kd1, b1)
    oa = _attention(q, k, v)
    x = _mixout(x, mod, hf, hr, yb, yf, yr, g, bonus, oa, lp["w_mix_out"], lp["rwkv_ln_g"], lp["rwkv_ln_b"])
    x = _ffn(x, mod, 2, lp["norm_g"][2], lp["ffn_w_in"][1], lp["ffn_w_out"][1])
    return x


def kernel(x_prompt, x_sample, c_prompt, c_sample, w_ada, b_ada, norm_g, ffn_w_in, ffn_w_out, w_mix_in, w_mix_out, lru_conv_w, lru_conv_b, lru_w_gate_a, lru_b_gate_a, lru_w_gate_x, lru_b_gate_x, lru_lambda, rwkv_mu, rwkv_w_up, rwkv_w0, rwkv_a_up, rwkv_a0, rwkv_g_up, rwkv_k_k, rwkv_k_a, rwkv_r_k, rwkv_ln_g, rwkv_ln_b, attn_q_norm, attn_k_norm):
    depth = w_ada.shape[0]
    bp, bs = x_prompt.shape[0], x_sample.shape[0]
    rows = -(-(bp + bs) // SUBLANES_V7X) * SUBLANES_V7X
    c_all = jnp.concatenate([c_prompt, c_sample, jnp.zeros((rows - bp - bs, D_MODEL), F32)], axis=0)
    mod_all = _ada_mod(c_all, w_ada, b_ada)

    params = {
        "norm_g": norm_g, "ffn_w_in": ffn_w_in.astype(BF16), "ffn_w_out": ffn_w_out.astype(BF16),
        "w_mix_in": w_mix_in.astype(BF16), "w_mix_out": w_mix_out.astype(BF16),
        "lru_conv_w": lru_conv_w, "lru_conv_b": lru_conv_b, "rwkv_mu": rwkv_mu, "rwkv_w_up": rwkv_w_up,
        "rwkv_w0": rwkv_w0, "rwkv_a_up": rwkv_a_up, "rwkv_a0": rwkv_a0, "rwkv_g_up": rwkv_g_up,
        "rwkv_k_k": rwkv_k_k, "rwkv_k_a": rwkv_k_a, "rwkv_r_k": rwkv_r_k, "rwkv_ln_g": rwkv_ln_g,
        "rwkv_ln_b": rwkv_ln_b, "attn_q_norm": attn_q_norm, "attn_k_norm": attn_k_norm,
    }
    outs = []
    for x, lo, nb in ((x_prompt, 0, bp), (x_sample, bp, bs)):
        rope = _rope_tables(x.shape[1])
        for l in range(depth):
            lp = {name: arr[l] for name, arr in params.items()}
            lp["lru_gates"] = _lru_gate_weights(lru_w_gate_a[l], lru_b_gate_a[l], lru_w_gate_x[l],
                                                lru_b_gate_x[l], lru_lambda[l])
            mod = mod_all[l, lo:lo + nb].reshape(nb, N_MOD, D_MODEL)
            x = _layer(x, mod, lp, rope)
        outs.append(x)
    return tuple(outs)
```

```python
import functools
import math

import jax
import jax.numpy as jnp
from jax import lax
from jax.experimental import pallas as pl
from jax.experimental.pallas import tpu as pltpu

F32 = jnp.float32
BF16 = jnp.bfloat16

D_MODEL = 1024
D_FF = 2816
N_MOD = 9
GRID_W = 64
LRU_W = 384
LRU_BLOCKS = 6
LRU_BS = LRU_W // LRU_BLOCKS
LRU_C = 8.0
RWKV_HEADS = 4
RWKV_HD = 64
RWKV_W = RWKV_HEADS * RWKV_HD
W_LORA = 64
A_LORA = 64
G_LORA = 128
RWKV_IN = 3 * RWKV_W + W_LORA + A_LORA + G_LORA
ATT_HEADS = 6
ATT_KV = 2
ATT_G = ATT_HEADS // ATT_KV
ATT_HD = 64
ATT_Q = ATT_HEADS * ATT_HD
ATT_KVW = ATT_KV * ATT_HD
ROPE_THETA = 10000.0
ROPE_PAIRS = ATT_HD // 4
D_MIX = LRU_W + RWKV_W + ATT_Q
D_IN = 2 * LRU_W + RWKV_IN + ATT_Q + 2 * ATT_KVW
NORM_EPS = 1e-6
GN_EPS = 64e-5

LANES_V7X = 128
SUBLANES_V7X = 8
VMEM_LIMIT_BYTES = 56 * 1024 * 1024

FFN_TOKENS = 512
FFN_COLS = 256
MIX_TOKENS = 512
SEQ_TILE = 256
SCAN_TILE = 256
RWKV_CHUNK = 64
ATT_Q_TILE = 256
ATT_KV_TILE = 2048
ATT_KV_SUB = 256
ATT_VT_ROWS = 80
LOG2_E = 1.4426950408889634


def _cparams(sem):
    return pltpu.CompilerParams(dimension_semantics=sem, vmem_limit_bytes=VMEM_LIMIT_BYTES)


def _dot(a, b):
    return jnp.dot(a.astype(BF16), b.astype(BF16), preferred_element_type=F32)


def _dot_nt(a, b):
    return lax.dot_general(a.astype(BF16), b.astype(BF16), (((1,), (1,)), ((), ())),
                           preferred_element_type=F32)


def _dot_tn(a, b):
    return lax.dot_general(a.astype(BF16), b.astype(BF16), (((0,), (0,)), ((), ())),
                           preferred_element_type=F32)


def _dot_split(x, w, terms):
    acc = None
    rem = x
    for _ in range(terms):
        hi = rem.astype(BF16)
        part = jnp.dot(hi, w, preferred_element_type=F32)
        acc = part if acc is None else acc + part
        rem = rem - hi.astype(F32)
    return acc


def _sigmoid(x):
    return 1.0 / (1.0 + jnp.exp(-x))


def _silu(x):
    return x * _sigmoid(x)


def _softplus(x):
    return jnp.maximum(x, 0.0) + jnp.log(1.0 + jnp.exp(-jnp.abs(x)))


def _gelu_tanh(x):
    c = math.sqrt(2.0 / math.pi)
    return 0.5 * x * (1.0 + jnp.tanh(c * (x + 0.044715 * (x * x * x))))


def _rms_mod(x, g, scale, shift):
    ms = jnp.mean(x * x, axis=-1, keepdims=True)
    y = x * lax.rsqrt(ms + NORM_EPS) * g
    return y * (1.0 + scale) + shift


def _ada_kernel(c_ref, w_ref, b_ref, o_ref):
    c = c_ref[...]
    o_ref[0] = _dot(_silu(c), w_ref[0]) + b_ref[0]


def _ada_mod(c_all, w_ada, b_ada):
    depth = w_ada.shape[0]
    rows = c_all.shape[0]
    ncol = N_MOD * D_MODEL
    tn = 1152
    return pl.pallas_call(
        _ada_kernel,
        name="ada_mod",
        out_shape=jax.ShapeDtypeStruct((depth, rows, ncol), F32),
        grid=(depth, ncol // tn),
        in_specs=[
            pl.BlockSpec((rows, D_MODEL), lambda l, j: (0, 0)),
            pl.BlockSpec((1, D_MODEL, tn), lambda l, j: (l, 0, j)),
            pl.BlockSpec((1, 1, tn), lambda l, j: (l, 0, j)),
        ],
        out_specs=pl.BlockSpec((1, rows, tn), lambda l, j: (l, 0, j)),
        compiler_params=_cparams(("parallel", "parallel")),
    )(c_all, w_ada, b_ada.reshape(depth, 1, ncol))


def _ffn_kernel(x_ref, mod_ref, g_ref, wi_ref, wo_ref, o_ref, *, sub, tf):
    x = x_ref[...]
    h = _rms_mod(x, g_ref[...], mod_ref[0, 3 * sub + 1:3 * sub + 2, :], mod_ref[0, 3 * sub:3 * sub + 1, :])
    h = h.astype(BF16)
    n_chunks = D_FF // tf

    def gate_up(c):
        return (jnp.dot(h, wi_ref[:, c * tf:(c + 1) * tf], preferred_element_type=F32),
                jnp.dot(h, wi_ref[:, D_FF + c * tf:D_FF + (c + 1) * tf], preferred_element_type=F32))

    pending = gate_up(0)
    acc = None
    for c in range(n_chunks):
        gate, up = pending
        if c + 1 < n_chunks:
            pending = gate_up(c + 1)
        act = (_silu(gate) * up).astype(BF16)
        part = jnp.dot(act, wo_ref[c * tf:(c + 1) * tf, :], preferred_element_type=F32)
        acc = part if acc is None else acc + part
    o_ref[...] = x + 0.5 * mod_ref[0, 3 * sub + 2:3 * sub + 3, :] * acc


def _resident(a):
    return pl.BlockSpec(a.shape, lambda *_: (0,) * a.ndim, pipeline_mode=pl.Buffered(1))


def _ffn(x, mod, sub, g, w_in, w_out):
    b, s, _ = x.shape
    n = b * s
    tm = min(FFN_TOKENS, s)
    g2 = g.reshape(1, D_MODEL)
    out = pl.pallas_call(
        functools.partial(_ffn_kernel, sub=sub, tf=FFN_COLS),
        name="ffn",
        out_shape=jax.ShapeDtypeStruct((n, D_MODEL), F32),
        grid=(n // tm,),
        in_specs=[
            pl.BlockSpec((tm, D_MODEL), lambda i: (i, 0)),
            pl.BlockSpec((1, N_MOD, D_MODEL), lambda i: ((i * tm) // s, 0, 0)),
            _resident(g2), _resident(w_in), _resident(w_out),
        ],
        out_specs=pl.BlockSpec((tm, D_MODEL), lambda i: (i, 0)),
        compiler_params=_cparams(("parallel",)),
    )(x.reshape(n, D_MODEL), mod, g2, w_in, w_out)
    return out.reshape(b, s, D_MODEL)


def _rope_rotate(x, cos, sin_signed):
    outs = []
    for c in range(x.shape[1] // LANES_V7X):
        sl = slice(c * LANES_V7X, (c + 1) * LANES_V7X)
        xc = x[:, sl]
        lane = lax.broadcasted_iota(jnp.int32, xc.shape, 1)
        first = (lane % 32) < 16
        partner = jnp.where(first, pltpu.roll(xc, LANES_V7X - 16, 1), pltpu.roll(xc, 16, 1))
        outs.append(xc * cos[:, sl] + partner * sin_signed[:, sl])
    return jnp.concatenate(outs, axis=1) if len(outs) > 1 else outs[0]


def _rwkv_token_mix(z, prev_row, next_row, mu_ref, ones_ref, wup_ref, aup_ref, gup_ref, w0_ref, a0_ref,
                    kk_w_ref, ka_ref, rk_ref, outs):
    r_ref, v_ref, kk_ref, g_ref, bonus_ref, lw0_ref, kd0_ref, b0_ref, lw1_ref, kd1_ref, b1_ref = outs
    t = z.shape[0]
    row = lax.broadcasted_iota(jnp.int32, z.shape, 0)
    prev = jnp.where(row == 0, prev_row, _shift_rows(z, 1))
    nxt = jnp.where(row == t - 1, next_row, _shift_rows(z, -1))
    f = z + mu_ref[...] * (0.5 * (prev + nxt) - z)

    w = RWKV_W
    r = f[:, 0:w]
    k = f[:, w:2 * w]
    v = f[:, 2 * w:3 * w]
    lora = f[:, 3 * w:3 * w + W_LORA + A_LORA]
    xg = f[:, 3 * w + W_LORA + A_LORA:]
    ones = ones_ref[...]

    g_ref[...] = _dot(_sigmoid(xg), gup_ref[...]).astype(g_ref.dtype)
    kk = k * kk_w_ref[...]
    ssq = _dot_split(kk * kk, ones, 2)
    kk = kk * lax.rsqrt(jnp.maximum(ssq, 1e-24))
    a_lora = _dot(lora, aup_ref[...])
    w_lora = jnp.tanh(lora)
    r_ref[...] = r
    v_ref[...] = v.astype(v_ref.dtype)
    kk_ref[...] = kk
    bonus_ref[...] = (_dot_split(r * k * rk_ref[...], ones, 2) * v).astype(bonus_ref.dtype)
    ka = ka_ref[...]
    for d, (lw_ref, kd_ref, b_ref) in enumerate(((lw0_ref, kd0_ref, b0_ref), (lw1_ref, kd1_ref, b1_ref))):
        u = w0_ref[d] + _dot(w_lora, wup_ref[d])
        lw_ref[...] = -jnp.exp(-_softplus(-u) - 0.5)
        a = _sigmoid(a0_ref[d] + a_lora)
        kd_ref[...] = k * (1.0 + (a - 1.0) * ka)
        b_ref[...] = kk * a


def _mixin_kernel(x_ref, xp_ref, xn_ref, mod_ref, g_ref, w_ref, onesq_ref, gq_ref, gk_ref, cq_ref, sq_ref, ck_ref,
                  sk_ref, mu_ref, onesr_ref, wup_ref, aup_ref, gup_ref, w0_ref, a0_ref, kk_w_ref, ka_ref, rk_ref,
                  xb_ref, yb_ref, q_ref, k_ref, v_ref, *rwkv_outs, n_tiles):
    scale, shift = mod_ref[0, 4:5, :], mod_ref[0, 3:4, :]
    h = _rms_mod(x_ref[...], g_ref[...], scale, shift).astype(BF16)
    halo = jnp.concatenate([xp_ref[...], xn_ref[...]], axis=0)
    h_halo = _rms_mod(halo, g_ref[...], scale, shift).astype(BF16)

    def proj(lo, hi):
        return jnp.dot(h, w_ref[:, lo:hi], preferred_element_type=F32)

    o1 = 2 * LRU_W + RWKV_IN
    qkv = proj(o1, D_IN)
    q = qkv[:, :ATT_Q]
    k = qkv[:, ATT_Q:ATT_Q + ATT_KVW]
    v = qkv[:, ATT_Q + ATT_KVW:]
    z = proj(2 * LRU_W, o1)
    z_halo = jnp.dot(h_halo, w_ref[:, 2 * LRU_W:o1], preferred_element_type=F32)

    ones = onesq_ref[...]
    ssq_q = _dot_split(q * q, ones, 2)
    qn = q * lax.rsqrt(ssq_q * (1.0 / ATT_HD) + NORM_EPS) * gq_ref[...]
    qr = _rope_rotate(qn, cq_ref[...], sq_ref[...]) * (ATT_HD ** -0.5 * LOG2_E)
    for hh in range(ATT_HEADS):
        q_ref[0, hh] = qr[:, hh * ATT_HD:(hh + 1) * ATT_HD].astype(BF16)

    pos = pl.program_id(0) % n_tiles
    sub = SUBLANES_V7X
    prev_row = jnp.where(pos > 0, z_halo[sub - 1:sub], 0.0)
    next_row = jnp.where(pos < n_tiles - 1, z_halo[sub:sub + 1], 0.0)
    _rwkv_token_mix(z, prev_row, next_row, mu_ref, onesr_ref, wup_ref, aup_ref, gup_ref, w0_ref, a0_ref,
                    kk_w_ref, ka_ref, rk_ref, rwkv_outs)

    xy = proj(0, 2 * LRU_W)
    xb_ref[...] = xy[:, :LRU_W]
    yb_ref[...] = xy[:, LRU_W:].astype(yb_ref.dtype)
    ssq_k = _dot_split(k * k, ones[:ATT_KVW, :ATT_KVW], 2)
    kn = k * lax.rsqrt(ssq_k * (1.0 / ATT_HD) + NORM_EPS) * gk_ref[...]
    kr = _rope_rotate(kn, ck_ref[...], sk_ref[...])
    for hh in range(ATT_KV):
        k_ref[0, hh] = kr[:, hh * ATT_HD:(hh + 1) * ATT_HD].astype(BF16)
        v_ref[0, hh] = v[:, hh * ATT_HD:(hh + 1) * ATT_HD].astype(BF16)


def _rope_tables(seq):
    n_rows = seq // GRID_W
    row = jnp.repeat(jnp.arange(n_rows, dtype=F32), GRID_W)
    col = jnp.tile(jnp.arange(GRID_W, dtype=F32), n_rows)
    inv = ROPE_THETA ** (-jnp.arange(ROPE_PAIRS, dtype=F32) / ROPE_PAIRS)
    ang_r = row[:, None] * inv
    ang_c = col[:, None] * inv
    cos_h = jnp.concatenate([jnp.cos(ang_r), jnp.cos(ang_r), jnp.cos(ang_c), jnp.cos(ang_c)], axis=1)
    sin_h = jnp.concatenate([-jnp.sin(ang_r), jnp.sin(ang_r), -jnp.sin(ang_c), jnp.sin(ang_c)], axis=1)
    return (jnp.tile(cos_h, (1, ATT_HEADS)), jnp.tile(sin_h, (1, ATT_HEADS)),
            jnp.tile(cos_h, (1, ATT_KV)), jnp.tile(sin_h, (1, ATT_KV)))


def _block_ones(width, block):
    idx = jnp.arange(width)
    return (idx[:, None] // block == idx[None, :] // block).astype(BF16)


RWKV_STREAMS = 11
RWKV_BF16_STREAMS = (1, 3, 4)


def _mixin(x, mod, g, w_in, gq, gk, rope, rwkv):
    b, s, _ = x.shape
    n = b * s
    tm = min(MIX_TOKENS, s)
    nt = s // tm
    r8 = tm // SUBLANES_V7X
    last8 = n // SUBLANES_V7X - 1
    cq, sq, ck, sk = rope
    tok = lambda w: pl.BlockSpec((tm, w), lambda i: (i, 0))
    pos = lambda w: pl.BlockSpec((tm, w), lambda i: (i % nt, 0))
    full = lambda a: pl.BlockSpec(a.shape, lambda i: (0,) * a.ndim)
    ones = _block_ones(ATT_Q, ATT_HD)
    gq_t = jnp.tile(gq.reshape(1, ATT_HD), (1, ATT_HEADS))
    gk_t = jnp.tile(gk.reshape(1, ATT_HD), (1, ATT_KV))
    g2 = g.reshape(1, D_MODEL)
    head = lambda nh: pl.BlockSpec((1, nh, tm, ATT_HD), lambda i: (i // nt, 0, i % nt, 0))
    consts = [g2, w_in, ones, gq_t, gk_t]
    x2 = x.reshape(n, D_MODEL)
    rwkv_dtypes = [BF16 if i in RWKV_BF16_STREAMS else F32 for i in range(RWKV_STREAMS)]
    outs = pl.pallas_call(
        functools.partial(_mixin_kernel, n_tiles=nt),
        name="mix_in",
        out_shape=[
            jax.ShapeDtypeStruct((n, LRU_W), F32),
            jax.ShapeDtypeStruct((n, LRU_W), BF16),
            jax.ShapeDtypeStruct((b, ATT_HEADS, s, ATT_HD), BF16),
            jax.ShapeDtypeStruct((b, ATT_KV, s, ATT_HD), BF16),
            jax.ShapeDtypeStruct((b, ATT_KV, s, ATT_HD), BF16),
        ] + [jax.ShapeDtypeStruct((n, RWKV_W), dt) for dt in rwkv_dtypes],
        grid=(n // tm,),
        in_specs=[
            tok(D_MODEL),
            pl.BlockSpec((SUBLANES_V7X, D_MODEL), lambda i: (jnp.maximum(i * r8 - 1, 0), 0)),
            pl.BlockSpec((SUBLANES_V7X, D_MODEL), lambda i: (jnp.minimum((i + 1) * r8, last8), 0)),
            pl.BlockSpec((1, N_MOD, D_MODEL), lambda i: ((i * tm) // s, 0, 0)),
        ] + [full(c) for c in consts] + [pos(ATT_Q), pos(ATT_Q), pos(ATT_KVW), pos(ATT_KVW)]
        + [full(c) for c in rwkv],
        out_specs=[tok(LRU_W), tok(LRU_W), head(ATT_HEADS), head(ATT_KV), head(ATT_KV)]
        + [tok(RWKV_W)] * RWKV_STREAMS,
        compiler_params=_cparams(("parallel",)),
    )(x2, x2, x2, mod, *consts, cq, sq, ck, sk, *rwkv)
    xb, yb, q, k, v = outs[:5]
    streams = [a.reshape(b, s, RWKV_W) for a in outs[5:]]
    return xb.reshape(b, s, LRU_W), yb.reshape(b, s, LRU_W), q, k, v, streams


def _rwkv_consts(mu, w_up, w0, a_up, a0, g_up, k_k, k_a, r_k):
    zeros = jnp.zeros((2, W_LORA, RWKV_W), F32)
    wup = jnp.concatenate([w_up, zeros], axis=1).astype(BF16)
    aup = jnp.concatenate([jnp.zeros((W_LORA, RWKV_W), F32), a_up], axis=0).astype(BF16)
    return [mu.reshape(1, RWKV_IN), _block_ones(RWKV_W, RWKV_HD), wup, aup, g_up.astype(BF16),
            w0.reshape(2, 1, RWKV_W), a0.reshape(2, 1, RWKV_W), k_k.reshape(1, RWKV_W),
            k_a.reshape(1, RWKV_W), r_k.reshape(1, RWKV_W)]


def _attn_kernel(q_ref, k_ref, vt_ref, o_ref, m_ref, acc_ref, *, tq, nk):
    ki = pl.program_id(2)

    @pl.when(ki == 0)
    def _():
        m_ref[...] = jnp.full_like(m_ref, -jnp.inf)
        acc_ref[...] = jnp.zeros_like(acc_ref)

    tk = k_ref.shape[2]
    sub = min(ATT_KV_SUB, tk)
    qs = [q_ref[0, ATT_G * g:ATT_G * (g + 1)].reshape(ATT_G * tq, ATT_HD) for g in range(ATT_KV)]
    jobs = [(g, c * sub) for c in range(tk // sub) for g in range(ATT_KV)]

    def scores(job):
        g, lo = job
        return lax.dot_general(k_ref[0, g, lo:lo + sub, :], qs[g], (((1,), (1,)), ((), ())),
                               preferred_element_type=F32)

    ahead = 2
    pending = {i: scores(jobs[i]) for i in range(min(ahead, len(jobs)))}
    for i, (g, lo) in enumerate(jobs):
        if i + ahead < len(jobs):
            pending[i + ahead] = scores(jobs[i + ahead])
        st = pending.pop(i)
        m_prev = m_ref[g]
        m_new = jnp.maximum(m_prev, jnp.max(st, axis=0, keepdims=True))
        p = jnp.exp2(st - m_new).astype(BF16)
        acc_ref[g] = jnp.exp2(m_prev - m_new) * acc_ref[g] + jnp.dot(vt_ref[0, g, :, lo:lo + sub], p,
                                                                      preferred_element_type=F32)
        m_ref[g] = m_new

    @pl.when(ki == nk - 1)
    def _():
        heads = []
        for g in range(ATT_KV):
            acc = acc_ref[g]
            o = (acc[:ATT_HD] / acc[ATT_HD:ATT_HD + 1]).T
            heads += [o[hh * tq:(hh + 1) * tq] for hh in range(ATT_G)]
        o_ref[0] = jnp.concatenate(heads, axis=1).astype(BF16)


def _attention(q, k, v):
    b, _, s, _ = q.shape
    ones_rows = jnp.ones((b, ATT_KV, ATT_VT_ROWS - ATT_HD, s), BF16)
    vt = jnp.concatenate([jnp.swapaxes(v, 2, 3), ones_rows], axis=2)
    tq = min(ATT_Q_TILE, s)
    tk = min(ATT_KV_TILE, s)
    nk = s // tk
    return pl.pallas_call(
        functools.partial(_attn_kernel, tq=tq, nk=nk),
        name="attention",
        out_shape=jax.ShapeDtypeStruct((b, s, ATT_Q), BF16),
        grid=(b, s // tq, nk),
        in_specs=[
            pl.BlockSpec((1, ATT_HEADS, tq, ATT_HD), lambda bi, qi, ki: (bi, 0, qi, 0)),
            pl.BlockSpec((1, ATT_KV, tk, ATT_HD), lambda bi, qi, ki: (bi, 0, ki, 0)),
            pl.BlockSpec((1, ATT_KV, ATT_VT_ROWS, tk), lambda bi, qi, ki: (bi, 0, 0, ki)),
        ],
        out_specs=pl.BlockSpec((1, tq, ATT_Q), lambda bi, qi, ki: (bi, qi, 0)),
        scratch_shapes=[
            pltpu.VMEM((ATT_KV, 1, ATT_G * tq), F32),
            pltpu.VMEM((ATT_KV, ATT_VT_ROWS, ATT_G * tq), F32),
        ],
        compiler_params=_cparams(("parallel", "parallel", "arbitrary")),
    )(q, k, vt)


def _shift_rows(x, shift):
    return pltpu.roll(x, shift % x.shape[0], 0)


def _lru_direction(x_ref, xp_ref, xn_ref, cw_ref, cb_ref, wg_ref, bg_ref, lam_ref, h_ref, carry_ref, ext_ref,
                   tile, n_tiles, rev):
    x = x_ref[0]
    t = x.shape[0]
    grp = SUBLANES_V7X
    ext_ref[0:grp, :] = jnp.where(tile > 0, xp_ref[0], 0.0)
    ext_ref[grp:grp + t, :] = x
    ext_ref[grp + t:, :] = jnp.where(tile < n_tiles - 1, xn_ref[0], 0.0)
    cw = cw_ref[...]
    xc = (cw[0:1] * ext_ref[grp - 2:grp - 2 + t, :] + cw[1:2] * ext_ref[grp - 1:grp - 1 + t, :] + cw[2:3] * x
          + cw[3:4] * ext_ref[grp + 1:grp + 1 + t, :] + cb_ref[...])

    gates = _dot(xc, wg_ref[...]) + bg_ref[...]
    r = _sigmoid(gates[:, :LRU_W])
    gi = _sigmoid(gates[:, LRU_W:])
    log_a = (-LRU_C) * r * _softplus(-lam_ref[...])
    a = jnp.exp(log_a)
    u = jnp.sqrt(1.0 - a * a) * (gi * xc)

    a = a.reshape(t // grp, grp, LRU_W)
    u = u.reshape(t // grp, grp, LRU_W)
    pos = lax.broadcasted_iota(jnp.int32, a.shape, 1)
    step = 1
    while step < grp:
        if not rev:
            valid = pos >= step
            shift = step
        else:
            valid = pos < grp - step
            shift = grp - step
        a_sh = jnp.where(valid, pltpu.roll(a, shift, 1), 1.0)
        u_sh = jnp.where(valid, pltpu.roll(u, shift, 1), 0.0)
        u = u + a * u_sh
        a = a * a_sh
        step *= 2
    a = a.reshape(t, LRU_W)
    u = u.reshape(t, LRU_W)
    carry = carry_ref[0:1, :]
    groups = range(t // grp)
    for gi in (reversed(groups) if rev else groups):
        sl = slice(gi * grp, (gi + 1) * grp)
        h = u[sl] + a[sl] * carry
        h_ref[0, sl, :] = h.astype(h_ref.dtype)
        carry = h[0:1] if rev else h[grp - 1:grp]
    carry_ref[...] = jnp.broadcast_to(carry, carry_ref.shape)


def _lru_kernel(xf_ref, xfp_ref, xfn_ref, xr_ref, xrp_ref, xrn_ref, cw_ref, cb_ref, wg_ref, bg_ref, lam_ref,
                hf_ref, hr_ref, cf_ref, cr_ref, ef_ref, er_ref, *, n_tiles):
    i = pl.program_id(1)

    @pl.when(i == 0)
    def _():
        cf_ref[...] = jnp.zeros_like(cf_ref)
        cr_ref[...] = jnp.zeros_like(cr_ref)

    _lru_direction(xf_ref, xfp_ref, xfn_ref, cw_ref, cb_ref, wg_ref.at[0], bg_ref.at[0], lam_ref.at[0],
                   hf_ref, cf_ref, ef_ref, i, n_tiles, False)
    _lru_direction(xr_ref, xrp_ref, xrn_ref, cw_ref, cb_ref, wg_ref.at[1], bg_ref.at[1], lam_ref.at[1],
                   hr_ref, cr_ref, er_ref, n_tiles - 1 - i, n_tiles, True)


def _halo_specs(width, t, n_tiles, s, rev):
    r8 = t // SUBLANES_V7X
    last8 = s // SUBLANES_V7X - 1
    tile = (lambda i: n_tiles - 1 - i) if rev else (lambda i: i)
    return [
        pl.BlockSpec((1, t, width), lambda b, i: (b, tile(i), 0)),
        pl.BlockSpec((1, SUBLANES_V7X, width), lambda b, i: (b, jnp.maximum(tile(i) * r8 - 1, 0), 0)),
        pl.BlockSpec((1, SUBLANES_V7X, width), lambda b, i: (b, jnp.minimum((tile(i) + 1) * r8, last8), 0)),
    ]


def _lru(xb, conv_w, conv_b, wg, bg, lam):
    b, s, _ = xb.shape
    t = min(SEQ_TILE, s)
    n_tiles = s // t
    full = lambda a: pl.BlockSpec(a.shape, lambda bi, i: (0,) * a.ndim)
    cb = conv_b.reshape(1, LRU_W)
    out_f = pl.BlockSpec((1, t, LRU_W), lambda bi, i: (bi, i, 0))
    out_r = pl.BlockSpec((1, t, LRU_W), lambda bi, i: (bi, n_tiles - 1 - i, 0))
    return pl.pallas_call(
        functools.partial(_lru_kernel, n_tiles=n_tiles),
        name="rglru",
        out_shape=[jax.ShapeDtypeStruct((b, s, LRU_W), BF16)] * 2,
        grid=(b, n_tiles),
        in_specs=_halo_specs(LRU_W, t, n_tiles, s, False) + _halo_specs(LRU_W, t, n_tiles, s, True)
        + [full(conv_w), full(cb), full(wg), full(bg), full(lam)],
        out_specs=[out_f, out_r],
        scratch_shapes=[pltpu.VMEM((SUBLANES_V7X, LRU_W), F32)] * 2
        + [pltpu.VMEM((t + 2 * SUBLANES_V7X, LRU_W), F32)] * 2,
        compiler_params=_cparams(("parallel", "arbitrary")),
    )(xb, xb, xb, xb, xb, xb, conv_w, cb, wg, bg, lam)


def _lru_gate_weights(w_a, b_a, w_x, b_x, lam):
    def bd(w):
        eye = jnp.eye(LRU_BLOCKS, dtype=w.dtype)
        return jnp.einsum("dnij,nm->dnimj", w, eye).reshape(2, LRU_W, LRU_W)
    wg = jnp.concatenate([bd(w_a), bd(w_x)], axis=-1).astype(BF16)
    bg = jnp.concatenate([b_a, b_x], axis=-1).reshape(2, 1, 2 * LRU_W)
    return wg, bg, lam.reshape(2, 1, LRU_W)


def _rwkv_masks(rev, c):
    hc = RWKV_HEADS * c
    rows = lax.broadcasted_iota(jnp.int32, (hc, hc), 0)
    cols = lax.broadcasted_iota(jnp.int32, (hc, hc), 1)
    lane = lax.broadcasted_iota(jnp.int32, (hc, RWKV_W), 1)
    srow = lax.broadcasted_iota(jnp.int32, (hc, RWKV_W), 0)
    ti = lax.broadcasted_iota(jnp.int32, (c, c), 0)
    tj = lax.broadcasted_iota(jnp.int32, (c, c), 1)
    if rev:
        strict, incl, tri, last = cols > rows, cols >= rows, tj >= ti, 0
    else:
        strict, incl, tri, last = cols < rows, cols <= rows, tj <= ti, c - 1
    return dict(strict=strict, incl=incl, tri=jnp.where(tri, 1.0, 0.0).astype(BF16), last=last,
                eye=rows == cols, head_mask=(srow // c) == (lane // RWKV_HD),
                same_block=lambda size: (rows // size) == (cols // size))


def _rwkv_tile(directions, chunk):
    c = chunk
    hc = RWKV_HEADS * c
    t = directions[0][0][0].shape[1]
    n_chunks = t // c
    jobs = []
    for step in range(n_chunks):
        for refs, y_ref, st_ref, rev in directions:
            ci = n_chunks - 1 - step if rev else step
            jobs.append(dict(refs=refs, y_ref=y_ref, st_ref=st_ref, sl=pl.ds(ci * c, c), m=_rwkv_masks(rev, c)))

    for j in jobs:
        m = j["m"]

        def stack(x, m=m):
            return jnp.where(m["head_mask"], jnp.concatenate([x] * RWKV_HEADS, axis=0), 0.0)

        r, v, kk, lw, kd, b = (ref[0, j["sl"], :] for ref in j["refs"])
        cum = _tri_cumsum(m["tri"], lw)
        inv_decay = jnp.exp(-cum)
        cum_last = cum[m["last"]:m["last"] + 1]
        to_end = jnp.exp(cum_last - cum)
        j["decay_end"] = jnp.exp(cum_last)
        j["a_t"] = stack(-kk * jnp.exp(cum - lw)).astype(BF16)
        j["r_t"] = stack(r * jnp.exp(cum)).astype(BF16)
        j["bk"] = jnp.concatenate([stack(b * inv_decay), stack(kd * inv_decay)], axis=0).astype(BF16)
        j["v_s"] = stack(v).astype(BF16)
        j["b_end"] = stack(b * to_end).astype(BF16)
        j["k_end"] = stack(kd * to_end).astype(BF16)

    for j in jobs:
        m = j["m"]
        g1 = _dot_nt(j["a_t"], j["bk"])
        j["a_ab"] = jnp.where(m["strict"], g1[:, :hc], 0.0)
        j["a_ak"] = jnp.where(m["strict"], g1[:, hc:], 0.0).astype(BF16)
        g2 = _dot_nt(j["r_t"], j.pop("bk"))
        j["a_rb"] = jnp.where(m["incl"], g2[:, :hc], 0.0).astype(BF16)
        j["a_rk"] = jnp.where(m["incl"], g2[:, hc:], 0.0).astype(BF16)
        j["tinv"] = jnp.where(m["eye"], 1.0, jnp.where(m["same_block"](2), j["a_ab"], 0.0))

    size = 2
    while size < c:
        for j in jobs:
            m = j["m"]
            off = jnp.where(m["same_block"](2 * size) & jnp.logical_not(m["same_block"](size)), j["a_ab"], 0.0)
            tinv = j["tinv"].astype(BF16)
            j["tinv"] = j["tinv"] + jnp.dot(tinv, _dot(off, tinv).astype(BF16), preferred_element_type=F32)
        size *= 2

    for j in jobs:
        j.pop("a_ab")
        w1 = _dot(j.pop("a_ak"), j["v_s"])
        j["x"] = _dot(j.pop("tinv"), jnp.concatenate([j.pop("a_t"), w1.astype(BF16)], axis=1)).astype(BF16)

    for j in jobs:
        x = j.pop("x")
        z = _dot(j.pop("a_rb"), x)
        j["r_bar"] = (j.pop("r_t").astype(F32) + z[:, :RWKV_W]).astype(BF16)
        j["y0"] = z[:, RWKV_W:] + _dot(j.pop("a_rk"), j["v_s"])
        gh = _dot_tn(j.pop("b_end"), x)
        j["trans"] = (jnp.where(j["m"]["eye"], j.pop("decay_end"), 0.0) + gh[:, :RWKV_W]).astype(BF16)
        j["h0"] = gh[:, RWKV_W:] + _dot_tn(j.pop("k_end"), j.pop("v_s"))

    for j in jobs:
        st = j["st_ref"][...].astype(BF16)
        y_s = jnp.dot(j["r_bar"], st, preferred_element_type=F32) + j["y0"]
        j["st_ref"][...] = jnp.dot(j["trans"], st, preferred_element_type=F32) + j["h0"]
        y = y_s[0:c]
        for hh in range(1, RWKV_HEADS):
            y = y + y_s[hh * c:(hh + 1) * c]
        j["y_ref"][0, j["sl"], :] = y


def _tri_cumsum(tri, x):
    acc = None
    rem = x
    for _ in range(3):
        hi = rem.astype(BF16)
        part = jnp.dot(tri, hi, preferred_element_type=F32)
        acc = part if acc is None else acc + part
        rem = rem - hi.astype(F32)
    return acc


def _rwkv_scan_kernel(rf_ref, vf_ref, kkf_ref, lwf_ref, kdf_ref, bf_ref, rr_ref, vr_ref, kkr_ref, lwr_ref, kdr_ref,
                      br_ref, yf_ref, yr_ref, sf_ref, sr_ref, *, chunk):
    @pl.when(pl.program_id(1) == 0)
    def _():
        sf_ref[...] = jnp.zeros_like(sf_ref)
        sr_ref[...] = jnp.zeros_like(sr_ref)

    _rwkv_tile([((rf_ref, vf_ref, kkf_ref, lwf_ref, kdf_ref, bf_ref), yf_ref, sf_ref, False),
                ((rr_ref, vr_ref, kkr_ref, lwr_ref, kdr_ref, br_ref), yr_ref, sr_ref, True)], chunk)


def _rwkv_scan(r, v, kk, lw0, kd0, b0, lw1, kd1, b1):
    b, s, _ = r.shape
    t = min(SCAN_TILE, s)
    chunk = min(RWKV_CHUNK, t)
    n_tiles = s // t
    fwd = pl.BlockSpec((1, t, RWKV_W), lambda bi, i: (bi, i, 0))
    bwd = pl.BlockSpec((1, t, RWKV_W), lambda bi, i: (bi, n_tiles - 1 - i, 0))
    return pl.pallas_call(
        functools.partial(_rwkv_scan_kernel, chunk=chunk),
        name="rwkv_scan",
        out_shape=[jax.ShapeDtypeStruct((b, s, RWKV_W), F32)] * 2,
        grid=(b, n_tiles),
        in_specs=[fwd] * 6 + [bwd] * 6,
        out_specs=[fwd, bwd],
        scratch_shapes=[pltpu.VMEM((RWKV_W, RWKV_W), F32)] * 2,
        compiler_params=_cparams(("parallel", "arbitrary")),
    )(r, v, kk, lw0, kd0, b0, r, v, kk, lw1, kd1, b1)


def _mixout_kernel(x_ref, mod_ref, hf_ref, hr_ref, yb_ref, yf_ref, yr_ref, g_ref, bonus_ref, oa_ref,
                   wl_ref, wr_ref, wa_ref, ones_ref, lng_ref, lnb_ref, o_ref):
    o_lru = (hf_ref[...].astype(F32) + hr_ref[...].astype(F32)) * _gelu_tanh(yb_ref[...].astype(F32))
    y = yf_ref[...] + yr_ref[...]
    ones = ones_ref[...]
    mu = _dot_split(y, ones, 3) * (1.0 / RWKV_HD)
    dlt = y - mu
    var = _dot_split(dlt * dlt, ones, 2) * (1.0 / RWKV_HD)
    yn = dlt * lax.rsqrt(var + GN_EPS) * lng_ref[...] + lnb_ref[...]
    o_rwkv = (yn + bonus_ref[...].astype(F32)) * g_ref[...].astype(F32)
    acc = (_dot(o_lru, wl_ref[...]) + _dot(o_rwkv, wr_ref[...])
           + jnp.dot(oa_ref[...], wa_ref[...], preferred_element_type=F32))
    o_ref[...] = x_ref[...] + mod_ref[0, 5:6, :] * acc


def _mixout(x, mod, hf, hr, yb, yf, yr, g, bonus, oa, w_out, ln_g, ln_b):
    b, s, _ = x.shape
    n = b * s
    tm = min(MIX_TOKENS, s)
    nt = s // tm
    tok = lambda w: pl.BlockSpec((tm, w), lambda i: (i, 0))
    full = lambda a: pl.BlockSpec(a.shape, lambda i: (0,) * a.ndim)
    wl = w_out[:LRU_W]
    wr = w_out[LRU_W:LRU_W + RWKV_W]
    wa = w_out[LRU_W + RWKV_W:]
    ones = _block_ones(RWKV_W, RWKV_HD)
    lng = ln_g.reshape(1, RWKV_W)
    lnb = ln_b.reshape(1, RWKV_W)
    flat = lambda a: a.reshape(n, a.shape[-1])
    out = pl.pallas_call(
        _mixout_kernel,
        name="mix_out",
        out_shape=jax.ShapeDtypeStruct((n, D_MODEL), F32),
        grid=(n // tm,),
        in_specs=[
            tok(D_MODEL),
            pl.BlockSpec((1, N_MOD, D_MODEL), lambda i: ((i * tm) // s, 0, 0)),
            tok(LRU_W), tok(LRU_W), tok(LRU_W), tok(RWKV_W), tok(RWKV_W), tok(RWKV_W), tok(RWKV_W),
            tok(ATT_Q),
            full(wl), full(wr), full(wa), full(ones), full(lng), full(lnb),
        ],
        out_specs=tok(D_MODEL),
        compiler_params=_cparams(("parallel",)),
    )(flat(x), mod, flat(hf), flat(hr), flat(yb), flat(yf), flat(yr), flat(g), flat(bonus), flat(oa),
      wl, wr, wa, ones, lng, lnb)
    return out.reshape(b, s, D_MODEL)


def _layer(x, mod, lp, rope):
    x = _ffn(x, mod, 0, lp["norm_g"][0], lp["ffn_w_in"][0], lp["ffn_w_out"][0])
    rwkv = _rwkv_consts(lp["rwkv_mu"], lp["rwkv_w_up"], lp["rwkv_w0"], lp["rwkv_a_up"], lp["rwkv_a0"],
                        lp["rwkv_g_up"], lp["rwkv_k_k"], lp["rwkv_k_a"], lp["rwkv_r_k"])
    xb, yb, q, k, v, streams = _mixin(x, mod, lp["norm_g"][1], lp["w_mix_in"], lp["attn_q_norm"],
                                      lp["attn_k_norm"], rope, rwkv)
    r, vv, kk, g, bonus, lw0, kd0, b0, lw1, kd1, b1 = streams
    hf, hr = _lru(xb, lp["lru_conv_w"], lp["lru_conv_b"], *lp["lru_gates"])
    yf, yr = _rwkv_scan(r, vv, kk, lw0, kd0, b0, lw1, kd1, b1)
    oa = _attention(q, k, v)
    x = _mixout(x, mod, hf, hr, yb, yf, yr, g, bonus, oa, lp["w_mix_out"], lp["rwkv_ln_g"], lp["rwkv_ln_b"])
    x = _ffn(x, mod, 2, lp["norm_g"][2], lp["ffn_w_in"][1], lp["ffn_w_out"][1])
    return x


def kernel(x_prompt, x_sample, c_prompt, c_sample, w_ada, b_ada, norm_g, ffn_w_in, ffn_w_out, w_mix_in, w_mix_out, lru_conv_w, lru_conv_b, lru_w_gate_a, lru_b_gate_a, lru_w_gate_x, lru_b_gate_x, lru_lambda, rwkv_mu, rwkv_w_up, rwkv_w0, rwkv_a_up, rwkv_a0, rwkv_g_up, rwkv_k_k, rwkv_k_a, rwkv_r_k, rwkv_ln_g, rwkv_ln_b, attn_q_norm, attn_k_norm):
    depth = w_ada.shape[0]
    bp, bs = x_prompt.shape[0], x_sample.shape[0]
    rows = -(-(bp + bs) // SUBLANES_V7X) * SUBLANES_V7X
    c_all = jnp.concatenate([c_prompt, c_sample, jnp.zeros((rows - bp - bs, D_MODEL), F32)], axis=0)
    mod_all = _ada_mod(c_all, w_ada, b_ada)

    params = {
        "norm_g": norm_g, "ffn_w_in": ffn_w_in.astype(BF16), "ffn_w_out": ffn_w_out.astype(BF16),
        "w_mix_in": w_mix_in.astype(BF16), "w_mix_out": w_mix_out.astype(BF16),
        "lru_conv_w": lru_conv_w, "lru_conv_b": lru_conv_b, "rwkv_mu": rwkv_mu, "rwkv_w_up": rwkv_w_up,
        "rwkv_w0": rwkv_w0, "rwkv_a_up": rwkv_a_up, "rwkv_a0": rwkv_a0, "rwkv_g_up": rwkv_g_up,
        "rwkv_k_k": rwkv_k_k, "rwkv_k_a": rwkv_k_a, "rwkv_r_k": rwkv_r_k, "rwkv_ln_g": rwkv_ln_g,
        "rwkv_ln_b": rwkv_ln_b, "attn_q_norm": attn_q_norm, "attn_k_norm": attn_k_norm,
    }
    outs = []
    for x, lo, nb in ((x_prompt, 0, bp), (x_sample, bp, bs)):
        rope = _rope_tables(x.shape[1])
        for l in range(depth):
            lp = {name: arr[l] for name, arr in params.items()}
            lp["lru_gates"] = _lru_gate_weights(lru_w_gate_a[l], lru_b_gate_a[l], lru_w_gate_x[l],
                                                lru_b_gate_x[l], lru_lambda[l])
            mod = mod_all[l, lo:lo + nb].reshape(nb, N_MOD, D_MODEL)
            x = _layer(x, mod, lp, rope)
        outs.append(x)
    return tuple(outs)
```

```python
import functools
import math

import jax
import jax.numpy as jnp
from jax import lax
from jax.experimental import pallas as pl
from jax.experimental.pallas import tpu as pltpu

F32 = jnp.float32
BF16 = jnp.bfloat16

D_MODEL = 1024
D_FF = 2816
N_MOD = 9
GRID_W = 64
LRU_W = 384
LRU_BLOCKS = 6
LRU_BS = LRU_W // LRU_BLOCKS
LRU_C = 8.0
RWKV_HEADS = 4
RWKV_HD = 64
RWKV_W = RWKV_HEADS * RWKV_HD
W_LORA = 64
A_LORA = 64
G_LORA = 128
RWKV_IN = 3 * RWKV_W + W_LORA + A_LORA + G_LORA
ATT_HEADS = 6
ATT_KV = 2
ATT_G = ATT_HEADS // ATT_KV
ATT_HD = 64
ATT_Q = ATT_HEADS * ATT_HD
ATT_KVW = ATT_KV * ATT_HD
ROPE_THETA = 10000.0
ROPE_PAIRS = ATT_HD // 4
D_MIX = LRU_W + RWKV_W + ATT_Q
D_IN = 2 * LRU_W + RWKV_IN + ATT_Q + 2 * ATT_KVW
NORM_EPS = 1e-6
GN_EPS = 64e-5

LANES_V7X = 128
SUBLANES_V7X = 8
VMEM_LIMIT_BYTES = 56 * 1024 * 1024

FFN_TOKENS = 512
FFN_COLS = 256
MIX_TOKENS = 512
SEQ_TILE = 256
SCAN_TILE = 256
RWKV_CHUNK = 64
ATT_Q_TILE = 512
ATT_KV_TILE = 2048
ATT_KV_SUB = 256
ATT_VT_ROWS = 80
LOG2_E = 1.4426950408889634


def _cparams(sem):
    return pltpu.CompilerParams(dimension_semantics=sem, vmem_limit_bytes=VMEM_LIMIT_BYTES)


def _dot(a, b):
    return jnp.dot(a.astype(BF16), b.astype(BF16), preferred_element_type=F32)


def _dot_nt(a, b):
    return lax.dot_general(a.astype(BF16), b.astype(BF16), (((1,), (1,)), ((), ())),
                           preferred_element_type=F32)


def _dot_tn(a, b):
    return lax.dot_general(a.astype(BF16), b.astype(BF16), (((0,), (0,)), ((), ())),
                           preferred_element_type=F32)


def _dot_split(x, w, terms):
    acc = None
    rem = x
    for _ in range(terms):
        hi = rem.astype(BF16)
        part = jnp.dot(hi, w, preferred_element_type=F32)
        acc = part if acc is None else acc + part
        rem = rem - hi.astype(F32)
    return acc


def _sigmoid(x):
    return 1.0 / (1.0 + jnp.exp(-x))


def _silu(x):
    return x * _sigmoid(x)


def _softplus(x):
    return jnp.maximum(x, 0.0) + jnp.log(1.0 + jnp.exp(-jnp.abs(x)))


def _gelu_tanh(x):
    c = math.sqrt(2.0 / math.pi)
    return 0.5 * x * (1.0 + jnp.tanh(c * (x + 0.044715 * (x * x * x))))


def _rms_mod(x, g, scale, shift):
    ms = jnp.mean(x * x, axis=-1, keepdims=True)
    y = x * lax.rsqrt(ms + NORM_EPS) * g
    return y * (1.0 + scale) + shift


def _ada_kernel(c_ref, w_ref, b_ref, o_ref):
    c = c_ref[...]
    o_ref[0] = _dot(_silu(c), w_ref[0]) + b_ref[0]


def _ada_mod(c_all, w_ada, b_ada):
    depth = w_ada.shape[0]
    rows = c_all.shape[0]
    ncol = N_MOD * D_MODEL
    tn = 1152
    return pl.pallas_call(
        _ada_kernel,
        name="ada_mod",
        out_shape=jax.ShapeDtypeStruct((depth, rows, ncol), F32),
        grid=(depth, ncol // tn),
        in_specs=[
            pl.BlockSpec((rows, D_MODEL), lambda l, j: (0, 0)),
            pl.BlockSpec((1, D_MODEL, tn), lambda l, j: (l, 0, j)),
            pl.BlockSpec((1, 1, tn), lambda l, j: (l, 0, j)),
        ],
        out_specs=pl.BlockSpec((1, rows, tn), lambda l, j: (l, 0, j)),
        compiler_params=_cparams(("parallel", "parallel")),
    )(c_all, w_ada, b_ada.reshape(depth, 1, ncol))


def _ffn_body(x, mod_ref, g_ref, wi_ref, wo_ref, sub, tf):
    h = _rms_mod(x, g_ref[...], mod_ref[0, 3 * sub + 1:3 * sub + 2, :], mod_ref[0, 3 * sub:3 * sub + 1, :])
    h = h.astype(BF16)
    n_chunks = D_FF // tf

    def gate_up(c):
        return (jnp.dot(h, wi_ref[:, c * tf:(c + 1) * tf], preferred_element_type=F32),
                jnp.dot(h, wi_ref[:, D_FF + c * tf:D_FF + (c + 1) * tf], preferred_element_type=F32))

    pending = gate_up(0)
    acc = None
    for c in range(n_chunks):
        gate, up = pending
        if c + 1 < n_chunks:
            pending = gate_up(c + 1)
        act = (_silu(gate) * up).astype(BF16)
        part = jnp.dot(act, wo_ref[c * tf:(c + 1) * tf, :], preferred_element_type=F32)
        acc = part if acc is None else acc + part
    return x + 0.5 * mod_ref[0, 3 * sub + 2:3 * sub + 3, :] * acc


def _ffn_kernel(x_ref, mod_ref, g_ref, wi_ref, wo_ref, o_ref, *, sub, tf):
    o_ref[...] = _ffn_body(x_ref[...], mod_ref, g_ref, wi_ref, wo_ref, sub, tf)


def _resident(a):
    return pl.BlockSpec(a.shape, lambda *_: (0,) * a.ndim, pipeline_mode=pl.Buffered(1))


def _ffn(x, mod, sub, g, w_in, w_out):
    b, s, _ = x.shape
    n = b * s
    tm = min(FFN_TOKENS, s)
    g2 = g.reshape(1, D_MODEL)
    out = pl.pallas_call(
        functools.partial(_ffn_kernel, sub=sub, tf=FFN_COLS),
        name="ffn",
        out_shape=jax.ShapeDtypeStruct((n, D_MODEL), F32),
        grid=(n // tm,),
        in_specs=[
            pl.BlockSpec((tm, D_MODEL), lambda i: (i, 0)),
            pl.BlockSpec((1, N_MOD, D_MODEL), lambda i: ((i * tm) // s, 0, 0)),
            _resident(g2), _resident(w_in), _resident(w_out),
        ],
        out_specs=pl.BlockSpec((tm, D_MODEL), lambda i: (i, 0)),
        compiler_params=_cparams(("parallel",)),
    )(x.reshape(n, D_MODEL), mod, g2, w_in, w_out)
    return out.reshape(b, s, D_MODEL)


def _rope_rotate(x, cos, sin_signed):
    outs = []
    for c in range(x.shape[1] // LANES_V7X):
        sl = slice(c * LANES_V7X, (c + 1) * LANES_V7X)
        xc = x[:, sl]
        lane = lax.broadcasted_iota(jnp.int32, xc.shape, 1)
        first = (lane % 32) < 16
        partner = jnp.where(first, pltpu.roll(xc, LANES_V7X - 16, 1), pltpu.roll(xc, 16, 1))
        outs.append(xc * cos[:, sl] + partner * sin_signed[:, sl])
    return jnp.concatenate(outs, axis=1) if len(outs) > 1 else outs[0]


def _rwkv_token_mix(z, prev_row, next_row, mu_ref, ones_ref, wup_ref, aup_ref, gup_ref, w0_ref, a0_ref,
                    kk_w_ref, ka_ref, rk_ref, outs):
    r_ref, v_ref, kk_ref, g_ref, bonus_ref, lw0_ref, kd0_ref, b0_ref, lw1_ref, kd1_ref, b1_ref = outs
    t = z.shape[0]
    row = lax.broadcasted_iota(jnp.int32, z.shape, 0)
    prev = jnp.where(row == 0, prev_row, _shift_rows(z, 1))
    nxt = jnp.where(row == t - 1, next_row, _shift_rows(z, -1))
    f = z + mu_ref[...] * (0.5 * (prev + nxt) - z)

    w = RWKV_W
    r = f[:, 0:w]
    k = f[:, w:2 * w]
    v = f[:, 2 * w:3 * w]
    lora = f[:, 3 * w:3 * w + W_LORA + A_LORA]
    xg = f[:, 3 * w + W_LORA + A_LORA:]
    ones = ones_ref[...]

    g_ref[...] = _dot(_sigmoid(xg), gup_ref[...]).astype(g_ref.dtype)
    kk = k * kk_w_ref[...]
    ssq = _dot_split(kk * kk, ones, 2)
    kk = kk * lax.rsqrt(jnp.maximum(ssq, 1e-24))
    a_lora = _dot(lora, aup_ref[...])
    w_lora = jnp.tanh(lora)
    r_ref[...] = r
    v_ref[...] = v.astype(v_ref.dtype)
    kk_ref[...] = kk
    bonus_ref[...] = (_dot_split(r * k * rk_ref[...], ones, 2) * v).astype(bonus_ref.dtype)
    ka = ka_ref[...]
    for d, (lw_ref, kd_ref, b_ref) in enumerate(((lw0_ref, kd0_ref, b0_ref), (lw1_ref, kd1_ref, b1_ref))):
        u = w0_ref[d] + _dot(w_lora, wup_ref[d])
        lw_ref[...] = -jnp.exp(-_softplus(-u) - 0.5)
        a = _sigmoid(a0_ref[d] + a_lora)
        kd_ref[...] = k * (1.0 + (a - 1.0) * ka)
        b_ref[...] = kk * a


def _mixin_kernel(x_ref, xp_ref, xn_ref, mod_ref, g_ref, w_ref, onesq_ref, gq_ref, gk_ref, cq_ref, sq_ref, ck_ref,
                  sk_ref, mu_ref, onesr_ref, wup_ref, aup_ref, gup_ref, w0_ref, a0_ref, kk_w_ref, ka_ref, rk_ref,
                  xb_ref, yb_ref, q_ref, k_ref, v_ref, *rwkv_outs, n_tiles):
    scale, shift = mod_ref[0, 4:5, :], mod_ref[0, 3:4, :]
    h = _rms_mod(x_ref[...], g_ref[...], scale, shift).astype(BF16)
    halo = jnp.concatenate([xp_ref[...], xn_ref[...]], axis=0)
    h_halo = _rms_mod(halo, g_ref[...], scale, shift).astype(BF16)

    def proj(lo, hi):
        return jnp.dot(h, w_ref[:, lo:hi], preferred_element_type=F32)

    o1 = 2 * LRU_W + RWKV_IN
    qkv = proj(o1, D_IN)
    q = qkv[:, :ATT_Q]
    k = qkv[:, ATT_Q:ATT_Q + ATT_KVW]
    v = qkv[:, ATT_Q + ATT_KVW:]
    z = proj(2 * LRU_W, o1)
    z_halo = jnp.dot(h_halo, w_ref[:, 2 * LRU_W:o1], preferred_element_type=F32)

    ones = onesq_ref[...]
    ssq_q = _dot_split(q * q, ones, 2)
    qn = q * lax.rsqrt(ssq_q * (1.0 / ATT_HD) + NORM_EPS) * gq_ref[...]
    qr = _rope_rotate(qn, cq_ref[...], sq_ref[...]) * (ATT_HD ** -0.5 * LOG2_E)
    for hh in range(ATT_HEADS):
        q_ref[0, hh] = qr[:, hh * ATT_HD:(hh + 1) * ATT_HD].astype(BF16)

    pos = pl.program_id(0) % n_tiles
    sub = SUBLANES_V7X
    prev_row = jnp.where(pos > 0, z_halo[sub - 1:sub], 0.0)
    next_row = jnp.where(pos < n_tiles - 1, z_halo[sub:sub + 1], 0.0)
    _rwkv_token_mix(z, prev_row, next_row, mu_ref, onesr_ref, wup_ref, aup_ref, gup_ref, w0_ref, a0_ref,
                    kk_w_ref, ka_ref, rk_ref, rwkv_outs)

    xy = proj(0, 2 * LRU_W)
    xb_ref[...] = xy[:, :LRU_W]
    yb_ref[...] = xy[:, LRU_W:].astype(yb_ref.dtype)
    ssq_k = _dot_split(k * k, ones[:ATT_KVW, :ATT_KVW], 2)
    kn = k * lax.rsqrt(ssq_k * (1.0 / ATT_HD) + NORM_EPS) * gk_ref[...]
    kr = _rope_rotate(kn, ck_ref[...], sk_ref[...])
    for hh in range(ATT_KV):
        k_ref[0, hh] = kr[:, hh * ATT_HD:(hh + 1) * ATT_HD].astype(BF16)
        v_ref[0, hh] = v[:, hh * ATT_HD:(hh + 1) * ATT_HD].astype(BF16)


def _rope_tables(seq):
    n_rows = seq // GRID_W
    row = jnp.repeat(jnp.arange(n_rows, dtype=F32), GRID_W)
    col = jnp.tile(jnp.arange(GRID_W, dtype=F32), n_rows)
    inv = ROPE_THETA ** (-jnp.arange(ROPE_PAIRS, dtype=F32) / ROPE_PAIRS)
    ang_r = row[:, None] * inv
    ang_c = col[:, None] * inv
    cos_h = jnp.concatenate([jnp.cos(ang_r), jnp.cos(ang_r), jnp.cos(ang_c), jnp.cos(ang_c)], axis=1)
    sin_h = jnp.concatenate([-jnp.sin(ang_r), jnp.sin(ang_r), -jnp.sin(ang_c), jnp.sin(ang_c)], axis=1)
    return (jnp.tile(cos_h, (1, ATT_HEADS)), jnp.tile(sin_h, (1, ATT_HEADS)),
            jnp.tile(cos_h, (1, ATT_KV)), jnp.tile(sin_h, (1, ATT_KV)))


def _block_ones(width, block):
    idx = jnp.arange(width)
    return (idx[:, None] // block == idx[None, :] // block).astype(BF16)


RWKV_STREAMS = 11
RWKV_BF16_STREAMS = (1, 3, 4)


def _mixin(x, mod, g, w_in, gq, gk, rope, rwkv):
    b, s, _ = x.shape
    n = b * s
    tm = min(MIX_TOKENS, s)
    nt = s // tm
    r8 = tm // SUBLANES_V7X
    last8 = n // SUBLANES_V7X - 1
    cq, sq, ck, sk = rope
    tok = lambda w: pl.BlockSpec((tm, w), lambda i: (i, 0))
    pos = lambda w: pl.BlockSpec((tm, w), lambda i: (i % nt, 0))
    full = lambda a: pl.BlockSpec(a.shape, lambda i: (0,) * a.ndim)
    ones = _block_ones(ATT_Q, ATT_HD)
    gq_t = jnp.tile(gq.reshape(1, ATT_HD), (1, ATT_HEADS))
    gk_t = jnp.tile(gk.reshape(1, ATT_HD), (1, ATT_KV))
    g2 = g.reshape(1, D_MODEL)
    head = lambda nh: pl.BlockSpec((1, nh, tm, ATT_HD), lambda i: (i // nt, 0, i % nt, 0))
    consts = [g2, w_in, ones, gq_t, gk_t]
    x2 = x.reshape(n, D_MODEL)
    rwkv_dtypes = [BF16 if i in RWKV_BF16_STREAMS else F32 for i in range(RWKV_STREAMS)]
    outs = pl.pallas_call(
        functools.partial(_mixin_kernel, n_tiles=nt),
        name="mix_in",
        out_shape=[
            jax.ShapeDtypeStruct((n, LRU_W), F32),
            jax.ShapeDtypeStruct((n, LRU_W), BF16),
            jax.ShapeDtypeStruct((b, ATT_HEADS, s, ATT_HD), BF16),
            jax.ShapeDtypeStruct((b, ATT_KV, s, ATT_HD), BF16),
            jax.ShapeDtypeStruct((b, ATT_KV, s, ATT_HD), BF16),
        ] + [jax.ShapeDtypeStruct((n, RWKV_W), dt) for dt in rwkv_dtypes],
        grid=(n // tm,),
        in_specs=[
            tok(D_MODEL),
            pl.BlockSpec((SUBLANES_V7X, D_MODEL), lambda i: (jnp.maximum(i * r8 - 1, 0), 0)),
            pl.BlockSpec((SUBLANES_V7X, D_MODEL), lambda i: (jnp.minimum((i + 1) * r8, last8), 0)),
            pl.BlockSpec((1, N_MOD, D_MODEL), lambda i: ((i * tm) // s, 0, 0)),
        ] + [full(c) for c in consts] + [pos(ATT_Q), pos(ATT_Q), pos(ATT_KVW), pos(ATT_KVW)]
        + [full(c) for c in rwkv],
        out_specs=[tok(LRU_W), tok(LRU_W), head(ATT_HEADS), head(ATT_KV), head(ATT_KV)]
        + [tok(RWKV_W)] * RWKV_STREAMS,
        compiler_params=_cparams(("parallel",)),
    )(x2, x2, x2, mod, *consts, cq, sq, ck, sk, *rwkv)
    xb, yb, q, k, v = outs[:5]
    streams = [a.reshape(b, s, RWKV_W) for a in outs[5:]]
    return xb.reshape(b, s, LRU_W), yb.reshape(b, s, LRU_W), q, k, v, streams


def _rwkv_consts(mu, w_up, w0, a_up, a0, g_up, k_k, k_a, r_k):
    zeros = jnp.zeros((2, W_LORA, RWKV_W), F32)
    wup = jnp.concatenate([w_up, zeros], axis=1).astype(BF16)
    aup = jnp.concatenate([jnp.zeros((W_LORA, RWKV_W), F32), a_up], axis=0).astype(BF16)
    return [mu.reshape(1, RWKV_IN), _block_ones(RWKV_W, RWKV_HD), wup, aup, g_up.astype(BF16),
            w0.reshape(2, 1, RWKV_W), a0.reshape(2, 1, RWKV_W), k_k.reshape(1, RWKV_W),
            k_a.reshape(1, RWKV_W), r_k.reshape(1, RWKV_W)]


def _attn_kernel(q_ref, k_ref, vt_ref, o_ref, m_ref, acc_ref, *, tq, nk):
    ki = pl.program_id(2)

    @pl.when(ki == 0)
    def _():
        m_ref[...] = jnp.full_like(m_ref, -jnp.inf)
        acc_ref[...] = jnp.zeros_like(acc_ref)

    tk = k_ref.shape[2]
    sub = min(ATT_KV_SUB, tk)
    qs = [q_ref[0, ATT_G * g:ATT_G * (g + 1)].reshape(ATT_G * tq, ATT_HD) for g in range(ATT_KV)]
    jobs = [(g, c * sub) for c in range(tk // sub) for g in range(ATT_KV)]

    def scores(job):
        g, lo = job
        return lax.dot_general(k_ref[0, g, lo:lo + sub, :], qs[g], (((1,), (1,)), ((), ())),
                               preferred_element_type=F32)

    ahead = 2
    pending = {i: scores(jobs[i]) for i in range(min(ahead, len(jobs)))}
    for i, (g, lo) in enumerate(jobs):
        if i + ahead < len(jobs):
            pending[i + ahead] = scores(jobs[i + ahead])
        st = pending.pop(i)
        m_prev = m_ref[g]
        m_new = jnp.maximum(m_prev, jnp.max(st, axis=0, keepdims=True))
        p = jnp.exp2(st - m_new).astype(BF16)
        acc_ref[g] = jnp.exp2(m_prev - m_new) * acc_ref[g] + jnp.dot(vt_ref[0, g, :, lo:lo + sub], p,
                                                                      preferred_element_type=F32)
        m_ref[g] = m_new

    @pl.when(ki == nk - 1)
    def _():
        heads = []
        for g in range(ATT_KV):
            acc = acc_ref[g]
            o = (acc[:ATT_HD] / acc[ATT_HD:ATT_HD + 1]).T
            heads += [o[hh * tq:(hh + 1) * tq] for hh in range(ATT_G)]
        o_ref[0] = jnp.concatenate(heads, axis=1).astype(BF16)


def _attention(q, k, v):
    b, _, s, _ = q.shape
    ones_rows = jnp.ones((b, ATT_KV, ATT_VT_ROWS - ATT_HD, s), BF16)
    vt = jnp.concatenate([jnp.swapaxes(v, 2, 3), ones_rows], axis=2)
    tq = min(ATT_Q_TILE, s)
    tk = min(ATT_KV_TILE, s)
    nk = s // tk
    return pl.pallas_call(
        functools.partial(_attn_kernel, tq=tq, nk=nk),
        name="attention",
        out_shape=jax.ShapeDtypeStruct((b, s, ATT_Q), BF16),
        grid=(b, s // tq, nk),
        in_specs=[
            pl.BlockSpec((1, ATT_HEADS, tq, ATT_HD), lambda bi, qi, ki: (bi, 0, qi, 0)),
            pl.BlockSpec((1, ATT_KV, tk, ATT_HD), lambda bi, qi, ki: (bi, 0, ki, 0)),
            pl.BlockSpec((1, ATT_KV, ATT_VT_ROWS, tk), lambda bi, qi, ki: (bi, 0, 0, ki)),
        ],
        out_specs=pl.BlockSpec((1, tq, ATT_Q), lambda bi, qi, ki: (bi, qi, 0)),
        scratch_shapes=[
            pltpu.VMEM((ATT_KV, 1, ATT_G * tq), F32),
            pltpu.VMEM((ATT_KV, ATT_VT_ROWS, ATT_G * tq), F32),
        ],
        compiler_params=_cparams(("parallel", "parallel", "arbitrary")),
    )(q, k, vt)


def _shift_rows(x, shift):
    return pltpu.roll(x, shift % x.shape[0], 0)


def _lru_direction(x_ref, xp_ref, xn_ref, cw_ref, cb_ref, wg_ref, bg_ref, lam_ref, h_ref, carry_ref, ext_ref,
                   tile, n_tiles, rev):
    x = x_ref[0]
    t = x.shape[0]
    grp = SUBLANES_V7X
    ext_ref[0:grp, :] = jnp.where(tile > 0, xp_ref[0], 0.0)
    ext_ref[grp:grp + t, :] = x
    ext_ref[grp + t:, :] = jnp.where(tile < n_tiles - 1, xn_ref[0], 0.0)
    cw = cw_ref[...]
    xc = (cw[0:1] * ext_ref[grp - 2:grp - 2 + t, :] + cw[1:2] * ext_ref[grp - 1:grp - 1 + t, :] + cw[2:3] * x
          + cw[3:4] * ext_ref[grp + 1:grp + 1 + t, :] + cb_ref[...])

    gates = _dot(xc, wg_ref[...]) + bg_ref[...]
    r = _sigmoid(gates[:, :LRU_W])
    gi = _sigmoid(gates[:, LRU_W:])
    log_a = (-LRU_C) * r * _softplus(-lam_ref[...])
    a = jnp.exp(log_a)
    u = jnp.sqrt(1.0 - a * a) * (gi * xc)

    a = a.reshape(t // grp, grp, LRU_W)
    u = u.reshape(t // grp, grp, LRU_W)
    pos = lax.broadcasted_iota(jnp.int32, a.shape, 1)
    step = 1
    while step < grp:
        if not rev:
            valid = pos >= step
            shift = step
        else:
            valid = pos < grp - step
            shift = grp - step
        a_sh = jnp.where(valid, pltpu.roll(a, shift, 1), 1.0)
        u_sh = jnp.where(valid, pltpu.roll(u, shift, 1), 0.0)
        u = u + a * u_sh
        a = a * a_sh
        step *= 2
    a = a.reshape(t, LRU_W)
    u = u.reshape(t, LRU_W)
    carry = carry_ref[0:1, :]
    groups = range(t // grp)
    for gi in (reversed(groups) if rev else groups):
        sl = slice(gi * grp, (gi + 1) * grp)
        h = u[sl] + a[sl] * carry
        h_ref[0, sl, :] = h.astype(h_ref.dtype)
        carry = h[0:1] if rev else h[grp - 1:grp]
    carry_ref[...] = jnp.broadcast_to(carry, carry_ref.shape)


def _lru_kernel(xf_ref, xfp_ref, xfn_ref, xr_ref, xrp_ref, xrn_ref, cw_ref, cb_ref, wg_ref, bg_ref, lam_ref,
                hf_ref, hr_ref, cf_ref, cr_ref, ef_ref, er_ref, *, n_tiles):
    i = pl.program_id(1)

    @pl.when(i == 0)
    def _():
        cf_ref[...] = jnp.zeros_like(cf_ref)
        cr_ref[...] = jnp.zeros_like(cr_ref)

    _lru_direction(xf_ref, xfp_ref, xfn_ref, cw_ref, cb_ref, wg_ref.at[0], bg_ref.at[0], lam_ref.at[0],
                   hf_ref, cf_ref, ef_ref, i, n_tiles, False)
    _lru_direction(xr_ref, xrp_ref, xrn_ref, cw_ref, cb_ref, wg_ref.at[1], bg_ref.at[1], lam_ref.at[1],
                   hr_ref, cr_ref, er_ref, n_tiles - 1 - i, n_tiles, True)


def _halo_specs(width, t, n_tiles, s, rev):
    r8 = t // SUBLANES_V7X
    last8 = s // SUBLANES_V7X - 1
    tile = (lambda i: n_tiles - 1 - i) if rev else (lambda i: i)
    return [
        pl.BlockSpec((1, t, width), lambda b, i: (b, tile(i), 0)),
        pl.BlockSpec((1, SUBLANES_V7X, width), lambda b, i: (b, jnp.maximum(tile(i) * r8 - 1, 0), 0)),
        pl.BlockSpec((1, SUBLANES_V7X, width), lambda b, i: (b, jnp.minimum((tile(i) + 1) * r8, last8), 0)),
    ]


def _lru(xb, conv_w, conv_b, wg, bg, lam):
    b, s, _ = xb.shape
    t = min(SEQ_TILE, s)
    n_tiles = s // t
    full = lambda a: pl.BlockSpec(a.shape, lambda bi, i: (0,) * a.ndim)
    cb = conv_b.reshape(1, LRU_W)
    out_f = pl.BlockSpec((1, t, LRU_W), lambda bi, i: (bi, i, 0))
    out_r = pl.BlockSpec((1, t, LRU_W), lambda bi, i: (bi, n_tiles - 1 - i, 0))
    return pl.pallas_call(
        functools.partial(_lru_kernel, n_tiles=n_tiles),
        name="rglru",
        out_shape=[jax.ShapeDtypeStruct((b, s, LRU_W), BF16)] * 2,
        grid=(b, n_tiles),
        in_specs=_halo_specs(LRU_W, t, n_tiles, s, False) + _halo_specs(LRU_W, t, n_tiles, s, True)
        + [full(conv_w), full(cb), full(wg), full(bg), full(lam)],
        out_specs=[out_f, out_r],
        scratch_shapes=[pltpu.VMEM((SUBLANES_V7X, LRU_W), F32)] * 2
        + [pltpu.VMEM((t + 2 * SUBLANES_V7X, LRU_W), F32)] * 2,
        compiler_params=_cparams(("parallel", "arbitrary")),
    )(xb, xb, xb, xb, xb, xb, conv_w, cb, wg, bg, lam)


def _lru_gate_weights(w_a, b_a, w_x, b_x, lam):
    def bd(w):
        eye = jnp.eye(LRU_BLOCKS, dtype=w.dtype)
        return jnp.einsum("dnij,nm->dnimj", w, eye).reshape(2, LRU_W, LRU_W)
    wg = jnp.concatenate([bd(w_a), bd(w_x)], axis=-1).astype(BF16)
    bg = jnp.concatenate([b_a, b_x], axis=-1).reshape(2, 1, 2 * LRU_W)
    return wg, bg, lam.reshape(2, 1, LRU_W)


def _rwkv_masks(rev, c):
    hc = RWKV_HEADS * c
    rows = lax.broadcasted_iota(jnp.int32, (hc, hc), 0)
    cols = lax.broadcasted_iota(jnp.int32, (hc, hc), 1)
    lane = lax.broadcasted_iota(jnp.int32, (hc, RWKV_W), 1)
    srow = lax.broadcasted_iota(jnp.int32, (hc, RWKV_W), 0)
    ti = lax.broadcasted_iota(jnp.int32, (c, c), 0)
    tj = lax.broadcasted_iota(jnp.int32, (c, c), 1)
    if rev:
        strict, incl, tri, last = cols > rows, cols >= rows, tj >= ti, 0
    else:
        strict, incl, tri, last = cols < rows, cols <= rows, tj <= ti, c - 1
    return dict(strict=strict, incl=incl, tri=jnp.where(tri, 1.0, 0.0).astype(BF16), last=last,
                eye=rows == cols, head_mask=(srow // c) == (lane // RWKV_HD),
                same_block=lambda size: (rows // size) == (cols // size))


def _rwkv_tile(directions, chunk):
    c = chunk
    hc = RWKV_HEADS * c
    t = directions[0][0][0].shape[1]
    n_chunks = t // c
    jobs = []
    for step in range(n_chunks):
        for refs, y_ref, st_ref, rev in directions:
            ci = n_chunks - 1 - step if rev else step
            jobs.append(dict(refs=refs, y_ref=y_ref, st_ref=st_ref, sl=pl.ds(ci * c, c), m=_rwkv_masks(rev, c)))

    for j in jobs:
        m = j["m"]

        def stack(x, m=m):
            return jnp.where(m["head_mask"], jnp.concatenate([x] * RWKV_HEADS, axis=0), 0.0)

        r, v, kk, lw, kd, b = (ref[0, j["sl"], :] for ref in j["refs"])
        cum = _tri_cumsum(m["tri"], lw)
        inv_decay = jnp.exp(-cum)
        cum_last = cum[m["last"]:m["last"] + 1]
        to_end = jnp.exp(cum_last - cum)
        j["decay_end"] = jnp.exp(cum_last)
        j["a_t"] = stack(-kk * jnp.exp(cum - lw)).astype(BF16)
        j["r_t"] = stack(r * jnp.exp(cum)).astype(BF16)
        j["bk"] = jnp.concatenate([stack(b * inv_decay), stack(kd * inv_decay)], axis=0).astype(BF16)
        j["v_s"] = stack(v).astype(BF16)
        j["b_end"] = stack(b * to_end).astype(BF16)
        j["k_end"] = stack(kd * to_end).astype(BF16)

    for j in jobs:
        m = j["m"]
        g1 = _dot_nt(j["a_t"], j["bk"])
        j["a_ab"] = jnp.where(m["strict"], g1[:, :hc], 0.0)
        j["a_ak"] = jnp.where(m["strict"], g1[:, hc:], 0.0).astype(BF16)
        g2 = _dot_nt(j["r_t"], j.pop("bk"))
        j["a_rb"] = jnp.where(m["incl"], g2[:, :hc], 0.0).astype(BF16)
        j["a_rk"] = jnp.where(m["incl"], g2[:, hc:], 0.0).astype(BF16)
        j["tinv"] = jnp.where(m["eye"], 1.0, jnp.where(m["same_block"](2), j["a_ab"], 0.0))

    size = 2
    while size < c:
        for j in jobs:
            m = j["m"]
            off = jnp.where(m["same_block"](2 * size) & jnp.logical_not(m["same_block"](size)), j["a_ab"], 0.0)
            tinv = j["tinv"].astype(BF16)
            j["tinv"] = j["tinv"] + jnp.dot(tinv, _dot(off, tinv).astype(BF16), preferred_element_type=F32)
        size *= 2

    for j in jobs:
        j.pop("a_ab")
        w1 = _dot(j.pop("a_ak"), j["v_s"])
        j["x"] = _dot(j.pop("tinv"), jnp.concatenate([j.pop("a_t"), w1.astype(BF16)], axis=1)).astype(BF16)

    for j in jobs:
        x = j.pop("x")
        z = _dot(j.pop("a_rb"), x)
        j["r_bar"] = (j.pop("r_t").astype(F32) + z[:, :RWKV_W]).astype(BF16)
        j["y0"] = z[:, RWKV_W:] + _dot(j.pop("a_rk"), j["v_s"])
        gh = _dot_tn(j.pop("b_end"), x)
        j["trans"] = (jnp.where(j["m"]["eye"], j.pop("decay_end"), 0.0) + gh[:, :RWKV_W]).astype(BF16)
        j["h0"] = gh[:, RWKV_W:] + _dot_tn(j.pop("k_end"), j.pop("v_s"))

    for j in jobs:
        st = j["st_ref"][...].astype(BF16)
        y_s = jnp.dot(j["r_bar"], st, preferred_element_type=F32) + j["y0"]
        j["st_ref"][...] = jnp.dot(j["trans"], st, preferred_element_type=F32) + j["h0"]
        y = y_s[0:c]
        for hh in range(1, RWKV_HEADS):
            y = y + y_s[hh * c:(hh + 1) * c]
        j["y_ref"][0, j["sl"], :] = y


def _tri_cumsum(tri, x):
    acc = None
    rem = x
    for _ in range(3):
        hi = rem.astype(BF16)
        part = jnp.dot(tri, hi, preferred_element_type=F32)
        acc = part if acc is None else acc + part
        rem = rem - hi.astype(F32)
    return acc


def _rwkv_scan_kernel(rf_ref, vf_ref, kkf_ref, lwf_ref, kdf_ref, bf_ref, rr_ref, vr_ref, kkr_ref, lwr_ref, kdr_ref,
                      br_ref, yf_ref, yr_ref, sf_ref, sr_ref, *, chunk):
    @pl.when(pl.program_id(1) == 0)
    def _():
        sf_ref[...] = jnp.zeros_like(sf_ref)
        sr_ref[...] = jnp.zeros_like(sr_ref)

    _rwkv_tile([((rf_ref, vf_ref, kkf_ref, lwf_ref, kdf_ref, bf_ref), yf_ref, sf_ref, False),
                ((rr_ref, vr_ref, kkr_ref, lwr_ref, kdr_ref, br_ref), yr_ref, sr_ref, True)], chunk)


def _rwkv_scan(r, v, kk, lw0, kd0, b0, lw1, kd1, b1):
    b, s, _ = r.shape
    t = min(SCAN_TILE, s)
    chunk = min(RWKV_CHUNK, t)
    n_tiles = s // t
    fwd = pl.BlockSpec((1, t, RWKV_W), lambda bi, i: (bi, i, 0))
    bwd = pl.BlockSpec((1, t, RWKV_W), lambda bi, i: (bi, n_tiles - 1 - i, 0))
    return pl.pallas_call(
        functools.partial(_rwkv_scan_kernel, chunk=chunk),
        name="rwkv_scan",
        out_shape=[jax.ShapeDtypeStruct((b, s, RWKV_W), F32)] * 2,
        grid=(b, n_tiles),
        in_specs=[fwd] * 6 + [bwd] * 6,
        out_specs=[fwd, bwd],
        scratch_shapes=[pltpu.VMEM((RWKV_W, RWKV_W), F32)] * 2,
        compiler_params=_cparams(("parallel", "arbitrary")),
    )(r, v, kk, lw0, kd0, b0, r, v, kk, lw1, kd1, b1)


def _mixout_ffn_kernel(x_ref, mod_ref, hf_ref, hr_ref, yb_ref, yf_ref, yr_ref, g_ref, bonus_ref, oa_ref,
                       wl_ref, wr_ref, wa_ref, ones_ref, lng_ref, lnb_ref, ng_ref, wi_ref, wo_ref, o_ref):
    o_lru = (hf_ref[...].astype(F32) + hr_ref[...].astype(F32)) * _gelu_tanh(yb_ref[...].astype(F32))
    y = yf_ref[...] + yr_ref[...]
    ones = ones_ref[...]
    mu = _dot_split(y, ones, 3) * (1.0 / RWKV_HD)
    dlt = y - mu
    var = _dot_split(dlt * dlt, ones, 2) * (1.0 / RWKV_HD)
    yn = dlt * lax.rsqrt(var + GN_EPS) * lng_ref[...] + lnb_ref[...]
    o_rwkv = (yn + bonus_ref[...].astype(F32)) * g_ref[...].astype(F32)
    acc = (_dot(o_lru, wl_ref[...]) + _dot(o_rwkv, wr_ref[...])
           + jnp.dot(oa_ref[...], wa_ref[...], preferred_element_type=F32))
    x_mid = x_ref[...] + mod_ref[0, 5:6, :] * acc
    o_ref[...] = _ffn_body(x_mid, mod_ref, ng_ref, wi_ref, wo_ref, 2, FFN_COLS)


def _mixout_ffn(x, mod, hf, hr, yb, yf, yr, g, bonus, oa, w_out, ln_g, ln_b, norm_g, w_ffn_in, w_ffn_out):
    b, s, _ = x.shape
    n = b * s
    tm = min(FFN_TOKENS, s)
    tok = lambda w: pl.BlockSpec((tm, w), lambda i: (i, 0))
    full = _resident
    ng = norm_g.reshape(1, D_MODEL)
    wl = w_out[:LRU_W]
    wr = w_out[LRU_W:LRU_W + RWKV_W]
    wa = w_out[LRU_W + RWKV_W:]
    ones = _block_ones(RWKV_W, RWKV_HD)
    lng = ln_g.reshape(1, RWKV_W)
    lnb = ln_b.reshape(1, RWKV_W)
    flat = lambda a: a.reshape(n, a.shape[-1])
    out = pl.pallas_call(
        _mixout_ffn_kernel,
        name="mix_out_ffn",
        out_shape=jax.ShapeDtypeStruct((n, D_MODEL), F32),
        grid=(n // tm,),
        in_specs=[
            tok(D_MODEL),
            pl.BlockSpec((1, N_MOD, D_MODEL), lambda i: ((i * tm) // s, 0, 0)),
            tok(LRU_W), tok(LRU_W), tok(LRU_W), tok(RWKV_W), tok(RWKV_W), tok(RWKV_W), tok(RWKV_W),
            tok(ATT_Q),
            full(wl), full(wr), full(wa), full(ones), full(lng), full(lnb),
            full(ng), full(w_ffn_in), full(w_ffn_out),
        ],
        out_specs=tok(D_MODEL),
        compiler_params=_cparams(("parallel",)),
    )(flat(x), mod, flat(hf), flat(hr), flat(yb), flat(yf), flat(yr), flat(g), flat(bonus), flat(oa),
      wl, wr, wa, ones, lng, lnb, ng, w_ffn_in, w_ffn_out)
    return out.reshape(b, s, D_MODEL)


def _layer(x, mod, lp, rope):
    x = _ffn(x, mod, 0, lp["norm_g"][0], lp["ffn_w_in"][0], lp["ffn_w_out"][0])
    rwkv = _rwkv_consts(lp["rwkv_mu"], lp["rwkv_w_up"], lp["rwkv_w0"], lp["rwkv_a_up"], lp["rwkv_a0"],
                        lp["rwkv_g_up"], lp["rwkv_k_k"], lp["rwkv_k_a"], lp["rwkv_r_k"])
    xb, yb, q, k, v, streams = _mixin(x, mod, lp["norm_g"][1], lp["w_mix_in"], lp["attn_q_norm"],
                                      lp["attn_k_norm"], rope, rwkv)
    r, vv, kk, g, bonus, lw0, kd0, b0, lw1, kd1, b1 = streams
    hf, hr = _lru(xb, lp["lru_conv_w"], lp["lru_conv_b"], *lp["lru_gates"])
    yf, yr = _rwkv_scan(r, vv, kk, lw0, kd0, b0, lw1, kd1, b1)
    oa = _attention(q, k, v)
    return _mixout_ffn(x, mod, hf, hr, yb, yf, yr, g, bonus, oa, lp["w_mix_out"], lp["rwkv_ln_g"],
                       lp["rwkv_ln_b"], lp["norm_g"][2], lp["ffn_w_in"][1], lp["ffn_w_out"][1])


def kernel(x_prompt, x_sample, c_prompt, c_sample, w_ada, b_ada, norm_g, ffn_w_in, ffn_w_out, w_mix_in, w_mix_out, lru_conv_w, lru_conv_b, lru_w_gate_a, lru_b_gate_a, lru_w_gate_x, lru_b_gate_x, lru_lambda, rwkv_mu, rwkv_w_up, rwkv_w0, rwkv_a_up, rwkv_a0, rwkv_g_up, rwkv_k_k, rwkv_k_a, rwkv_r_k, rwkv_ln_g, rwkv_ln_b, attn_q_norm, attn_k_norm):
    depth = w_ada.shape[0]
    bp, bs = x_prompt.shape[0], x_sample.shape[0]
    rows = -(-(bp + bs) // SUBLANES_V7X) * SUBLANES_V7X
    c_all = jnp.concatenate([c_prompt, c_sample, jnp.zeros((rows - bp - bs, D_MODEL), F32)], axis=0)
    mod_all = _ada_mod(c_all, w_ada, b_ada)

    params = {
        "norm_g": norm_g, "ffn_w_in": ffn_w_in.astype(BF16), "ffn_w_out": ffn_w_out.astype(BF16),
        "w_mix_in": w_mix_in.astype(BF16), "w_mix_out": w_mix_out.astype(BF16),
        "lru_conv_w": lru_conv_w, "lru_conv_b": lru_conv_b, "rwkv_mu": rwkv_mu, "rwkv_w_up": rwkv_w_up,
        "rwkv_w0": rwkv_w0, "rwkv_a_up": rwkv_a_up, "rwkv_a0": rwkv_a0, "rwkv_g_up": rwkv_g_up,
        "rwkv_k_k": rwkv_k_k, "rwkv_k_a": rwkv_k_a, "rwkv_r_k": rwkv_r_k, "rwkv_ln_g": rwkv_ln_g,
        "rwkv_ln_b": rwkv_ln_b, "attn_q_norm": attn_q_norm, "attn_k_norm": attn_k_norm,
    }
    outs = []
    for x, lo, nb in ((x_prompt, 0, bp), (x_sample, bp, bs)):
        rope = _rope_tables(x.shape[1])
        for l in range(depth):
            lp = {name: arr[l] for name, arr in params.items()}
            lp["lru_gates"] = _lru_gate_weights(lru_w_gate_a[l], lru_b_gate_a[l], lru_w_gate_x[l],
                                                lru_b_gate_x[l], lru_lambda[l])
            mod = mod_all[l, lo:lo + nb].reshape(nb, N_MOD, D_MODEL)
            x = _layer(x, mod, lp, rope)
        outs.append(x)
    return tuple(outs)
```

```python
import functools
import math

import jax
import jax.numpy as jnp
from jax import lax
from jax.experimental import pallas as pl
from jax.experimental.pallas import tpu as pltpu

F32 = jnp.float32
BF16 = jnp.bfloat16

D_MODEL = 1024
D_FF = 2816
N_MOD = 9
GRID_W = 64
LRU_W = 384
LRU_BLOCKS = 6
LRU_BS = LRU_W // LRU_BLOCKS
LRU_C = 8.0
RWKV_HEADS = 4
RWKV_HD = 64
RWKV_W = RWKV_HEADS * RWKV_HD
W_LORA = 64
A_LORA = 64
G_LORA = 128
RWKV_IN = 3 * RWKV_W + W_LORA + A_LORA + G_LORA
ATT_HEADS = 6
ATT_KV = 2
ATT_G = ATT_HEADS // ATT_KV
ATT_HD = 64
ATT_Q = ATT_HEADS * ATT_HD
ATT_KVW = ATT_KV * ATT_HD
ROPE_THETA = 10000.0
ROPE_PAIRS = ATT_HD // 4
D_MIX = LRU_W + RWKV_W + ATT_Q
D_IN = 2 * LRU_W + RWKV_IN + ATT_Q + 2 * ATT_KVW
NORM_EPS = 1e-6
GN_EPS = 64e-5

LANES_V7X = 128
SUBLANES_V7X = 8
VMEM_LIMIT_BYTES = 56 * 1024 * 1024

FFN_TOKENS = 512
FFN_COLS = 256
MIX_TOKENS = 512
SEQ_TILE = 256
SCAN_TILE = 256
RWKV_CHUNK = 64
ATT_Q_TILE = 256
ATT_KV_TILE = 2048
ATT_KV_SUB = 256
ATT_VT_ROWS = 80
LOG2_E = 1.4426950408889634


def _cparams(sem):
    return pltpu.CompilerParams(dimension_semantics=sem, vmem_limit_bytes=VMEM_LIMIT_BYTES)


def _dot(a, b):
    return jnp.dot(a.astype(BF16), b.astype(BF16), preferred_element_type=F32)


def _dot_nt(a, b):
    return lax.dot_general(a.astype(BF16), b.astype(BF16), (((1,), (1,)), ((), ())),
                           preferred_element_type=F32)


def _dot_tn(a, b):
    return lax.dot_general(a.astype(BF16), b.astype(BF16), (((0,), (0,)), ((), ())),
                           preferred_element_type=F32)


def _dot_split(x, w, terms):
    acc = None
    rem = x
    for _ in range(terms):
        hi = rem.astype(BF16)
        part = jnp.dot(hi, w, preferred_element_type=F32)
        acc = part if acc is None else acc + part
        rem = rem - hi.astype(F32)
    return acc


def _sigmoid(x):
    return 1.0 / (1.0 + jnp.exp(-x))


def _silu(x):
    return x * _sigmoid(x)


def _softplus(x):
    return jnp.maximum(x, 0.0) + jnp.log(1.0 + jnp.exp(-jnp.abs(x)))


def _gelu_tanh(x):
    c = math.sqrt(2.0 / math.pi)
    return 0.5 * x * (1.0 + jnp.tanh(c * (x + 0.044715 * (x * x * x))))


def _rms_mod(x, g, scale, shift):
    ms = jnp.mean(x * x, axis=-1, keepdims=True)
    y = x * lax.rsqrt(ms + NORM_EPS) * g
    return y * (1.0 + scale) + shift


def _ada_kernel(c_ref, w_ref, b_ref, o_ref):
    c = c_ref[...]
    o_ref[0] = _dot(_silu(c), w_ref[0]) + b_ref[0]


def _ada_mod(c_all, w_ada, b_ada):
    depth = w_ada.shape[0]
    rows = c_all.shape[0]
    ncol = N_MOD * D_MODEL
    tn = 1152
    return pl.pallas_call(
        _ada_kernel,
        name="ada_mod",
        out_shape=jax.ShapeDtypeStruct((depth, rows, ncol), F32),
        grid=(depth, ncol // tn),
        in_specs=[
            pl.BlockSpec((rows, D_MODEL), lambda l, j: (0, 0)),
            pl.BlockSpec((1, D_MODEL, tn), lambda l, j: (l, 0, j)),
            pl.BlockSpec((1, 1, tn), lambda l, j: (l, 0, j)),
        ],
        out_specs=pl.BlockSpec((1, rows, tn), lambda l, j: (l, 0, j)),
        compiler_params=_cparams(("parallel", "parallel")),
    )(c_all, w_ada, b_ada.reshape(depth, 1, ncol))


def _ffn_body(x, mod_ref, g_ref, wi_ref, wo_ref, sub, tf):
    h = _rms_mod(x, g_ref[...], mod_ref[0, 3 * sub + 1:3 * sub + 2, :], mod_ref[0, 3 * sub:3 * sub + 1, :])
    h = h.astype(BF16)
    n_chunks = D_FF // tf

    def gate_up(c):
        return (jnp.dot(h, wi_ref[:, c * tf:(c + 1) * tf], preferred_element_type=F32),
                jnp.dot(h, wi_ref[:, D_FF + c * tf:D_FF + (c + 1) * tf], preferred_element_type=F32))

    pending = gate_up(0)
    acc = None
    for c in range(n_chunks):
        gate, up = pending
        if c + 1 < n_chunks:
            pending = gate_up(c + 1)
        act = (_silu(gate) * up).astype(BF16)
        part = jnp.dot(act, wo_ref[c * tf:(c + 1) * tf, :], preferred_element_type=F32)
        acc = part if acc is None else acc + part
    return x + 0.5 * mod_ref[0, 3 * sub + 2:3 * sub + 3, :] * acc


def _ffn_kernel(x_ref, mod_ref, g_ref, wi_ref, wo_ref, o_ref, *, sub, tf):
    o_ref[...] = _ffn_body(x_ref[...], mod_ref, g_ref, wi_ref, wo_ref, sub, tf)


def _resident(a):
    return pl.BlockSpec(a.shape, lambda *_: (0,) * a.ndim, pipeline_mode=pl.Buffered(1))


def _ffn(x, mod, sub, g, w_in, w_out):
    b, s, _ = x.shape
    n = b * s
    tm = min(FFN_TOKENS, s)
    g2 = g.reshape(1, D_MODEL)
    out = pl.pallas_call(
        functools.partial(_ffn_kernel, sub=sub, tf=FFN_COLS),
        name="ffn",
        out_shape=jax.ShapeDtypeStruct((n, D_MODEL), F32),
        grid=(n // tm,),
        in_specs=[
            pl.BlockSpec((tm, D_MODEL), lambda i: (i, 0)),
            pl.BlockSpec((1, N_MOD, D_MODEL), lambda i: ((i * tm) // s, 0, 0)),
            _resident(g2), _resident(w_in), _resident(w_out),
        ],
        out_specs=pl.BlockSpec((tm, D_MODEL), lambda i: (i, 0)),
        compiler_params=_cparams(("parallel",)),
    )(x.reshape(n, D_MODEL), mod, g2, w_in, w_out)
    return out.reshape(b, s, D_MODEL)


def _rope_rotate(x, cos, sin_signed):
    outs = []
    for c in range(x.shape[1] // LANES_V7X):
        sl = slice(c * LANES_V7X, (c + 1) * LANES_V7X)
        xc = x[:, sl]
        lane = lax.broadcasted_iota(jnp.int32, xc.shape, 1)
        first = (lane % 32) < 16
        partner = jnp.where(first, pltpu.roll(xc, LANES_V7X - 16, 1), pltpu.roll(xc, 16, 1))
        outs.append(xc * cos[:, sl] + partner * sin_signed[:, sl])
    return jnp.concatenate(outs, axis=1) if len(outs) > 1 else outs[0]


def _rwkv_token_mix(z, prev_row, next_row, mu_ref, ones_ref, wup_ref, aup_ref, gup_ref, w0_ref, a0_ref,
                    kk_w_ref, ka_ref, rk_ref, outs):
    r_ref, v_ref, kk_ref, g_ref, bonus_ref, lw0_ref, kd0_ref, b0_ref, lw1_ref, kd1_ref, b1_ref = outs
    t = z.shape[0]
    row = lax.broadcasted_iota(jnp.int32, z.shape, 0)
    prev = jnp.where(row == 0, prev_row, _shift_rows(z, 1))
    nxt = jnp.where(row == t - 1, next_row, _shift_rows(z, -1))
    f = z + mu_ref[...] * (0.5 * (prev + nxt) - z)

    w = RWKV_W
    r = f[:, 0:w]
    k = f[:, w:2 * w]
    v = f[:, 2 * w:3 * w]
    lora = f[:, 3 * w:3 * w + W_LORA + A_LORA]
    xg = f[:, 3 * w + W_LORA + A_LORA:]
    ones = ones_ref[...]

    g_ref[...] = _dot(_sigmoid(xg), gup_ref[...]).astype(g_ref.dtype)
    kk = k * kk_w_ref[...]
    ssq = _dot_split(kk * kk, ones, 2)
    kk = kk * lax.rsqrt(jnp.maximum(ssq, 1e-24))
    a_lora = _dot(lora, aup_ref[...])
    w_lora = jnp.tanh(lora)
    r_ref[...] = r
    v_ref[...] = v.astype(v_ref.dtype)
    kk_ref[...] = kk
    bonus_ref[...] = (_dot_split(r * k * rk_ref[...], ones, 2) * v).astype(bonus_ref.dtype)
    ka = ka_ref[...]
    for d, (lw_ref, kd_ref, b_ref) in enumerate(((lw0_ref, kd0_ref, b0_ref), (lw1_ref, kd1_ref, b1_ref))):
        u = w0_ref[d] + _dot(w_lora, wup_ref[d])
        lw_ref[...] = -jnp.exp(-_softplus(-u) - 0.5)
        a = _sigmoid(a0_ref[d] + a_lora)
        kd_ref[...] = k * (1.0 + (a - 1.0) * ka)
        b_ref[...] = kk * a


def _mixin_kernel(x_ref, xp_ref, xn_ref, mod_ref, g_ref, w_ref, onesq_ref, gq_ref, gk_ref, cq_ref, sq_ref, ck_ref,
                  sk_ref, mu_ref, onesr_ref, wup_ref, aup_ref, gup_ref, w0_ref, a0_ref, kk_w_ref, ka_ref, rk_ref,
                  xb_ref, yb_ref, q_ref, k_ref, v_ref, *rwkv_outs, n_tiles):
    scale, shift = mod_ref[0, 4:5, :], mod_ref[0, 3:4, :]
    h = _rms_mod(x_ref[...], g_ref[...], scale, shift).astype(BF16)
    halo = jnp.concatenate([xp_ref[...], xn_ref[...]], axis=0)
    h_halo = _rms_mod(halo, g_ref[...], scale, shift).astype(BF16)

    def proj(lo, hi):
        return jnp.dot(h, w_ref[:, lo:hi], preferred_element_type=F32)

    o1 = 2 * LRU_W + RWKV_IN
    qkv = proj(o1, D_IN)
    q = qkv[:, :ATT_Q]
    k = qkv[:, ATT_Q:ATT_Q + ATT_KVW]
    v = qkv[:, ATT_Q + ATT_KVW:]
    z = proj(2 * LRU_W, o1)
    z_halo = jnp.dot(h_halo, w_ref[:, 2 * LRU_W:o1], preferred_element_type=F32)

    ones = onesq_ref[...]
    ssq_q = _dot_split(q * q, ones, 2)
    qn = q * lax.rsqrt(ssq_q * (1.0 / ATT_HD) + NORM_EPS) * gq_ref[...]
    qr = _rope_rotate(qn, cq_ref[...], sq_ref[...]) * (ATT_HD ** -0.5 * LOG2_E)
    for hh in range(ATT_HEADS):
        q_ref[0, hh] = qr[:, hh * ATT_HD:(hh + 1) * ATT_HD].astype(BF16)

    pos = pl.program_id(0) % n_tiles
    sub = SUBLANES_V7X
    prev_row = jnp.where(pos > 0, z_halo[sub - 1:sub], 0.0)
    next_row = jnp.where(pos < n_tiles - 1, z_halo[sub:sub + 1], 0.0)
    _rwkv_token_mix(z, prev_row, next_row, mu_ref, onesr_ref, wup_ref, aup_ref, gup_ref, w0_ref, a0_ref,
                    kk_w_ref, ka_ref, rk_ref, rwkv_outs)

    xy = proj(0, 2 * LRU_W)
    xb_ref[...] = xy[:, :LRU_W]
    yb_ref[...] = xy[:, LRU_W:].astype(yb_ref.dtype)
    ssq_k = _dot_split(k * k, ones[:ATT_KVW, :ATT_KVW], 2)
    kn = k * lax.rsqrt(ssq_k * (1.0 / ATT_HD) + NORM_EPS) * gk_ref[...]
    kr = _rope_rotate(kn, ck_ref[...], sk_ref[...])
    for hh in range(ATT_KV):
        k_ref[0, hh] = kr[:, hh * ATT_HD:(hh + 1) * ATT_HD].astype(BF16)
        v_ref[0, hh] = v[:, hh * ATT_HD:(hh + 1) * ATT_HD].astype(BF16)


def _rope_tables(seq):
    n_rows = seq // GRID_W
    row = jnp.repeat(jnp.arange(n_rows, dtype=F32), GRID_W)
    col = jnp.tile(jnp.arange(GRID_W, dtype=F32), n_rows)
    inv = ROPE_THETA ** (-jnp.arange(ROPE_PAIRS, dtype=F32) / ROPE_PAIRS)
    ang_r = row[:, None] * inv
    ang_c = col[:, None] * inv
    cos_h = jnp.concatenate([jnp.cos(ang_r), jnp.cos(ang_r), jnp.cos(ang_c), jnp.cos(ang_c)], axis=1)
    sin_h = jnp.concatenate([-jnp.sin(ang_r), jnp.sin(ang_r), -jnp.sin(ang_c), jnp.sin(ang_c)], axis=1)
    return (jnp.tile(cos_h, (1, ATT_HEADS)), jnp.tile(sin_h, (1, ATT_HEADS)),
            jnp.tile(cos_h, (1, ATT_KV)), jnp.tile(sin_h, (1, ATT_KV)))


def _block_ones(width, block):
    idx = jnp.arange(width)
    return (idx[:, None] // block == idx[None, :] // block).astype(BF16)


RWKV_STREAMS = 11
RWKV_BF16_STREAMS = (1, 3, 4)


def _mixin(x, mod, g, w_in, gq, gk, rope, rwkv):
    b, s, _ = x.shape
    n = b * s
    tm = min(MIX_TOKENS, s)
    nt = s // tm
    r8 = tm // SUBLANES_V7X
    last8 = n // SUBLANES_V7X - 1
    cq, sq, ck, sk = rope
    tok = lambda w: pl.BlockSpec((tm, w), lambda i: (i, 0))
    pos = lambda w: pl.BlockSpec((tm, w), lambda i: (i % nt, 0))
    full = lambda a: pl.BlockSpec(a.shape, lambda i: (0,) * a.ndim)
    ones = _block_ones(ATT_Q, ATT_HD)
    gq_t = jnp.tile(gq.reshape(1, ATT_HD), (1, ATT_HEADS))
    gk_t = jnp.tile(gk.reshape(1, ATT_HD), (1, ATT_KV))
    g2 = g.reshape(1, D_MODEL)
    head = lambda nh: pl.BlockSpec((1, nh, tm, ATT_HD), lambda i: (i // nt, 0, i % nt, 0))
    consts = [g2, w_in, ones, gq_t, gk_t]
    x2 = x.reshape(n, D_MODEL)
    rwkv_dtypes = [BF16 if i in RWKV_BF16_STREAMS else F32 for i in range(RWKV_STREAMS)]
    outs = pl.pallas_call(
        functools.partial(_mixin_kernel, n_tiles=nt),
        name="mix_in",
        out_shape=[
            jax.ShapeDtypeStruct((n, LRU_W), F32),
            jax.ShapeDtypeStruct((n, LRU_W), BF16),
            jax.ShapeDtypeStruct((b, ATT_HEADS, s, ATT_HD), BF16),
            jax.ShapeDtypeStruct((b, ATT_KV, s, ATT_HD), BF16),
            jax.ShapeDtypeStruct((b, ATT_KV, s, ATT_HD), BF16),
        ] + [jax.ShapeDtypeStruct((n, RWKV_W), dt) for dt in rwkv_dtypes],
        grid=(n // tm,),
        in_specs=[
            tok(D_MODEL),
            pl.BlockSpec((SUBLANES_V7X, D_MODEL), lambda i: (jnp.maximum(i * r8 - 1, 0), 0)),
            pl.BlockSpec((SUBLANES_V7X, D_MODEL), lambda i: (jnp.minimum((i + 1) * r8, last8), 0)),
            pl.BlockSpec((1, N_MOD, D_MODEL), lambda i: ((i * tm) // s, 0, 0)),
        ] + [full(c) for c in consts] + [pos(ATT_Q), pos(ATT_Q), pos(ATT_KVW), pos(ATT_KVW)]
        + [full(c) for c in rwkv],
        out_specs=[tok(LRU_W), tok(LRU_W), head(ATT_HEADS), head(ATT_KV), head(ATT_KV)]
        + [tok(RWKV_W)] * RWKV_STREAMS,
        compiler_params=_cparams(("parallel",)),
    )(x2, x2, x2, mod, *consts, cq, sq, ck, sk, *rwkv)
    xb, yb, q, k, v = outs[:5]
    streams = [a.reshape(b, s, RWKV_W) for a in outs[5:]]
    return xb.reshape(b, s, LRU_W), yb.reshape(b, s, LRU_W), q, k, v, streams


def _rwkv_consts(mu, w_up, w0, a_up, a0, g_up, k_k, k_a, r_k):
    zeros = jnp.zeros((2, W_LORA, RWKV_W), F32)
    wup = jnp.concatenate([w_up, zeros], axis=1).astype(BF16)
    aup = jnp.concatenate([jnp.zeros((W_LORA, RWKV_W), F32), a_up], axis=0).astype(BF16)
    return [mu.reshape(1, RWKV_IN), _block_ones(RWKV_W, RWKV_HD), wup, aup, g_up.astype(BF16),
            w0.reshape(2, 1, RWKV_W), a0.reshape(2, 1, RWKV_W), k_k.reshape(1, RWKV_W),
            k_a.reshape(1, RWKV_W), r_k.reshape(1, RWKV_W)]


def _attn_kernel(q_ref, k_ref, vt_ref, o_ref, m_ref, acc_ref, *, tq, nk):
    ki = pl.program_id(2)

    @pl.when(ki == 0)
    def _():
        m_ref[...] = jnp.full_like(m_ref, -jnp.inf)
        acc_ref[...] = jnp.zeros_like(acc_ref)

    tk = k_ref.shape[2]
    sub = min(ATT_KV_SUB, tk)
    qs = [q_ref[0, ATT_G * g:ATT_G * (g + 1)].reshape(ATT_G * tq, ATT_HD) for g in range(ATT_KV)]
    jobs = [(g, c * sub) for c in range(tk // sub) for g in range(ATT_KV)]

    def scores(job):
        g, lo = job
        return lax.dot_general(k_ref[0, g, lo:lo + sub, :], qs[g], (((1,), (1,)), ((), ())),
                               preferred_element_type=F32)

    ahead = 2
    pending = {i: scores(jobs[i]) for i in range(min(ahead, len(jobs)))}
    for i, (g, lo) in enumerate(jobs):
        if i + ahead < len(jobs):
            pending[i + ahead] = scores(jobs[i + ahead])
        st = pending.pop(i)
        m_prev = m_ref[g]
        m_new = jnp.maximum(m_prev, jnp.max(st, axis=0, keepdims=True))
        p = jnp.exp2(st - m_new).astype(BF16)
        acc_ref[g] = jnp.exp2(m_prev - m_new) * acc_ref[g] + jnp.dot(vt_ref[0, g, :, lo:lo + sub], p,
                                                                      preferred_element_type=F32)
        m_ref[g] = m_new

    @pl.when(ki == nk - 1)
    def _():
        heads = []
        for g in range(ATT_KV):
            acc = acc_ref[g]
            o = (acc[:ATT_HD] / acc[ATT_HD:ATT_HD + 1]).T
            heads += [o[hh * tq:(hh + 1) * tq] for hh in range(ATT_G)]
        o_ref[0] = jnp.concatenate(heads, axis=1).astype(BF16)


def _attention(q, k, v):
    b, _, s, _ = q.shape
    ones_rows = jnp.ones((b, ATT_KV, ATT_VT_ROWS - ATT_HD, s), BF16)
    vt = jnp.concatenate([jnp.swapaxes(v, 2, 3), ones_rows], axis=2)
    tq = min(ATT_Q_TILE, s)
    tk = min(ATT_KV_TILE, s)
    nk = s // tk
    return pl.pallas_call(
        functools.partial(_attn_kernel, tq=tq, nk=nk),
        name="attention",
        out_shape=jax.ShapeDtypeStruct((b, s, ATT_Q), BF16),
        grid=(b, s // tq, nk),
        in_specs=[
            pl.BlockSpec((1, ATT_HEADS, tq, ATT_HD), lambda bi, qi, ki: (bi, 0, qi, 0)),
            pl.BlockSpec((1, ATT_KV, tk, ATT_HD), lambda bi, qi, ki: (bi, 0, ki, 0)),
            pl.BlockSpec((1, ATT_KV, ATT_VT_ROWS, tk), lambda bi, qi, ki: (bi, 0, 0, ki)),
        ],
        out_specs=pl.BlockSpec((1, tq, ATT_Q), lambda bi, qi, ki: (bi, qi, 0)),
        scratch_shapes=[
            pltpu.VMEM((ATT_KV, 1, ATT_G * tq), F32),
            pltpu.VMEM((ATT_KV, ATT_VT_ROWS, ATT_G * tq), F32),
        ],
        compiler_params=_cparams(("parallel", "parallel", "arbitrary")),
    )(q, k, vt)


def _shift_rows(x, shift):
    return pltpu.roll(x, shift % x.shape[0], 0)


def _lru_direction(x_ref, xp_ref, xn_ref, cw_ref, cb_ref, wg_ref, bg_ref, lam_ref, h_ref, carry_ref, ext_ref,
                   tile, n_tiles, rev):
    x = x_ref[0]
    t = x.shape[0]
    grp = SUBLANES_V7X
    ext_ref[0:grp, :] = jnp.where(tile > 0, xp_ref[0], 0.0)
    ext_ref[grp:grp + t, :] = x
    ext_ref[grp + t:, :] = jnp.where(tile < n_tiles - 1, xn_ref[0], 0.0)
    cw = cw_ref[...]
    xc = (cw[0:1] * ext_ref[grp - 2:grp - 2 + t, :] + cw[1:2] * ext_ref[grp - 1:grp - 1 + t, :] + cw[2:3] * x
          + cw[3:4] * ext_ref[grp + 1:grp + 1 + t, :] + cb_ref[...])

    gates = _dot(xc, wg_ref[...]) + bg_ref[...]
    r = _sigmoid(gates[:, :LRU_W])
    gi = _sigmoid(gates[:, LRU_W:])
    log_a = (-LRU_C) * r * _softplus(-lam_ref[...])
    a = jnp.exp(log_a)
    u = jnp.sqrt(1.0 - a * a) * (gi * xc)

    a = a.reshape(t // grp, grp, LRU_W)
    u = u.reshape(t // grp, grp, LRU_W)
    pos = lax.broadcasted_iota(jnp.int32, a.shape, 1)
    step = 1
    while step < grp:
        if not rev:
            valid = pos >= step
            shift = step
        else:
            valid = pos < grp - step
            shift = grp - step
        a_sh = jnp.where(valid, pltpu.roll(a, shift, 1), 1.0)
        u_sh = jnp.where(valid, pltpu.roll(u, shift, 1), 0.0)
        u = u + a * u_sh
        a = a * a_sh
        step *= 2
    a = a.reshape(t, LRU_W)
    u = u.reshape(t, LRU_W)
    carry = carry_ref[0:1, :]
    groups = range(t // grp)
    for gi in (reversed(groups) if rev else groups):
        sl = slice(gi * grp, (gi + 1) * grp)
        h = u[sl] + a[sl] * carry
        h_ref[0, sl, :] = h.astype(h_ref.dtype)
        carry = h[0:1] if rev else h[grp - 1:grp]
    carry_ref[...] = jnp.broadcast_to(carry, carry_ref.shape)


def _lru_kernel(xf_ref, xfp_ref, xfn_ref, xr_ref, xrp_ref, xrn_ref, cw_ref, cb_ref, wg_ref, bg_ref, lam_ref,
                hf_ref, hr_ref, cf_ref, cr_ref, ef_ref, er_ref, *, n_tiles):
    i = pl.program_id(1)

    @pl.when(i == 0)
    def _():
        cf_ref[...] = jnp.zeros_like(cf_ref)
        cr_ref[...] = jnp.zeros_like(cr_ref)

    _lru_direction(xf_ref, xfp_ref, xfn_ref, cw_ref, cb_ref, wg_ref.at[0], bg_ref.at[0], lam_ref.at[0],
                   hf_ref, cf_ref, ef_ref, i, n_tiles, False)
    _lru_direction(xr_ref, xrp_ref, xrn_ref, cw_ref, cb_ref, wg_ref.at[1], bg_ref.at[1], lam_ref.at[1],
                   hr_ref, cr_ref, er_ref, n_tiles - 1 - i, n_tiles, True)


def _halo_specs(width, t, n_tiles, s, rev):
    r8 = t // SUBLANES_V7X
    last8 = s // SUBLANES_V7X - 1
    tile = (lambda i: n_tiles - 1 - i) if rev else (lambda i: i)
    return [
        pl.BlockSpec((1, t, width), lambda b, i: (b, tile(i), 0)),
        pl.BlockSpec((1, SUBLANES_V7X, width), lambda b, i: (b, jnp.maximum(tile(i) * r8 - 1, 0), 0)),
        pl.BlockSpec((1, SUBLANES_V7X, width), lambda b, i: (b, jnp.minimum((tile(i) + 1) * r8, last8), 0)),
    ]


def _lru(xb, conv_w, conv_b, wg, bg, lam):
    b, s, _ = xb.shape
    t = min(SEQ_TILE, s)
    n_tiles = s // t
    full = lambda a: pl.BlockSpec(a.shape, lambda bi, i: (0,) * a.ndim)
    cb = conv_b.reshape(1, LRU_W)
    out_f = pl.BlockSpec((1, t, LRU_W), lambda bi, i: (bi, i, 0))
    out_r = pl.BlockSpec((1, t, LRU_W), lambda bi, i: (bi, n_tiles - 1 - i, 0))
    return pl.pallas_call(
        functools.partial(_lru_kernel, n_tiles=n_tiles),
        name="rglru",
        out_shape=[jax.ShapeDtypeStruct((b, s, LRU_W), BF16)] * 2,
        grid=(b, n_tiles),
        in_specs=_halo_specs(LRU_W, t, n_tiles, s, False) + _halo_specs(LRU_W, t, n_tiles, s, True)
        + [full(conv_w), full(cb), full(wg), full(bg), full(lam)],
        out_specs=[out_f, out_r],
        scratch_shapes=[pltpu.VMEM((SUBLANES_V7X, LRU_W), F32)] * 2
        + [pltpu.VMEM((t + 2 * SUBLANES_V7X, LRU_W), F32)] * 2,
        compiler_params=_cparams(("parallel", "arbitrary")),
    )(xb, xb, xb, xb, xb, xb, conv_w, cb, wg, bg, lam)


def _lru_gate_weights(w_a, b_a, w_x, b_x, lam):
    def bd(w):
        eye = jnp.eye(LRU_BLOCKS, dtype=w.dtype)
        return jnp.einsum("dnij,nm->dnimj", w, eye).reshape(2, LRU_W, LRU_W)
    wg = jnp.concatenate([bd(w_a), bd(w_x)], axis=-1).astype(BF16)
    bg = jnp.concatenate([b_a, b_x], axis=-1).reshape(2, 1, 2 * LRU_W)
    return wg, bg, lam.reshape(2, 1, LRU_W)


def _rwkv_masks(rev, c):
    hc = RWKV_HEADS * c
    rows = lax.broadcasted_iota(jnp.int32, (hc, hc), 0)
    cols = lax.broadcasted_iota(jnp.int32, (hc, hc), 1)
    lane = lax.broadcasted_iota(jnp.int32, (hc, RWKV_W), 1)
    srow = lax.broadcasted_iota(jnp.int32, (hc, RWKV_W), 0)
    ti = lax.broadcasted_iota(jnp.int32, (c, c), 0)
    tj = lax.broadcasted_iota(jnp.int32, (c, c), 1)
    if rev:
        strict, incl, tri, last = cols > rows, cols >= rows, tj >= ti, 0
    else:
        strict, incl, tri, last = cols < rows, cols <= rows, tj <= ti, c - 1
    return dict(strict=strict, incl=incl, tri=jnp.where(tri, 1.0, 0.0).astype(BF16), last=last,
                eye=rows == cols, head_mask=(srow // c) == (lane // RWKV_HD),
                same_block=lambda size: (rows // size) == (cols // size))


def _rwkv_tile(directions, chunk):
    c = chunk
    hc = RWKV_HEADS * c
    t = directions[0][0][0].shape[1]
    n_chunks = t // c
    jobs = []
    for step in range(n_chunks):
        for refs, y_ref, st_ref, rev in directions:
            ci = n_chunks - 1 - step if rev else step
            jobs.append(dict(refs=refs, y_ref=y_ref, st_ref=st_ref, sl=pl.ds(ci * c, c), m=_rwkv_masks(rev, c)))

    for j in jobs:
        m = j["m"]

        def stack(x, m=m):
            return jnp.where(m["head_mask"], jnp.concatenate([x] * RWKV_HEADS, axis=0), 0.0)

        r, v, kk, lw, kd, b = (ref[0, j["sl"], :] for ref in j["refs"])
        cum = _tri_cumsum(m["tri"], lw)
        inv_decay = jnp.exp(-cum)
        cum_last = cum[m["last"]:m["last"] + 1]
        to_end = jnp.exp(cum_last - cum)
        j["decay_end"] = jnp.exp(cum_last)
        j["a_t"] = stack(-kk * jnp.exp(cum - lw)).astype(BF16)
        j["r_t"] = stack(r * jnp.exp(cum)).astype(BF16)
        j["bk"] = jnp.concatenate([stack(b * inv_decay), stack(kd * inv_decay)], axis=0).astype(BF16)
        j["v_s"] = stack(v).astype(BF16)
        j["b_end"] = stack(b * to_end).astype(BF16)
        j["k_end"] = stack(kd * to_end).astype(BF16)

    for j in jobs:
        m = j["m"]
        g1 = _dot_nt(j["a_t"], j["bk"])
        j["a_ab"] = jnp.where(m["strict"], g1[:, :hc], 0.0)
        j["a_ak"] = jnp.where(m["strict"], g1[:, hc:], 0.0).astype(BF16)
        g2 = _dot_nt(j["r_t"], j.pop("bk"))
        j["a_rb"] = jnp.where(m["incl"], g2[:, :hc], 0.0).astype(BF16)
        j["a_rk"] = jnp.where(m["incl"], g2[:, hc:], 0.0).astype(BF16)
        j["tinv"] = jnp.where(m["eye"], 1.0, jnp.where(m["same_block"](2), j["a_ab"], 0.0))

    size = 2
    while size < c:
        for j in jobs:
            m = j["m"]
            off = jnp.where(m["same_block"](2 * size) & jnp.logical_not(m["same_block"](size)), j["a_ab"], 0.0)
            tinv = j["tinv"].astype(BF16)
            j["tinv"] = j["tinv"] + jnp.dot(tinv, _dot(off, tinv).astype(BF16), preferred_element_type=F32)
        size *= 2

    for j in jobs:
        j.pop("a_ab")
        w1 = _dot(j.pop("a_ak"), j["v_s"])
        j["x"] = _dot(j.pop("tinv"), jnp.concatenate([j.pop("a_t"), w1.astype(BF16)], axis=1)).astype(BF16)

    for j in jobs:
        x = j.pop("x")
        z = _dot(j.pop("a_rb"), x)
        j["r_bar"] = (j.pop("r_t").astype(F32) + z[:, :RWKV_W]).astype(BF16)
        j["y0"] = z[:, RWKV_W:] + _dot(j.pop("a_rk"), j["v_s"])
        gh = _dot_tn(j.pop("b_end"), x)
        j["trans"] = (jnp.where(j["m"]["eye"], j.pop("decay_end"), 0.0) + gh[:, :RWKV_W]).astype(BF16)
        j["h0"] = gh[:, RWKV_W:] + _dot_tn(j.pop("k_end"), j.pop("v_s"))

    for j in jobs:
        st = j["st_ref"][...].astype(BF16)
        y_s = jnp.dot(j["r_bar"], st, preferred_element_type=F32) + j["y0"]
        j["st_ref"][...] = jnp.dot(j["trans"], st, preferred_element_type=F32) + j["h0"]
        y = y_s[0:c]
        for hh in range(1, RWKV_HEADS):
            y = y + y_s[hh * c:(hh + 1) * c]
        j["y_ref"][0, j["sl"], :] = y


def _tri_cumsum(tri, x):
    acc = None
    rem = x
    for _ in range(3):
        hi = rem.astype(BF16)
        part = jnp.dot(tri, hi, preferred_element_type=F32)
        acc = part if acc is None else acc + part
        rem = rem - hi.astype(F32)
    return acc


def _rwkv_scan_kernel(rf_ref, vf_ref, kkf_ref, lwf_ref, kdf_ref, bf_ref, rr_ref, vr_ref, kkr_ref, lwr_ref, kdr_ref,
                      br_ref, yf_ref, yr_ref, sf_ref, sr_ref, *, chunk):
    @pl.when(pl.program_id(1) == 0)
    def _():
        sf_ref[...] = jnp.zeros_like(sf_ref)
        sr_ref[...] = jnp.zeros_like(sr_ref)

    _rwkv_tile([((rf_ref, vf_ref, kkf_ref, lwf_ref, kdf_ref, bf_ref), yf_ref, sf_ref, False),
                ((rr_ref, vr_ref, kkr_ref, lwr_ref, kdr_ref, br_ref), yr_ref, sr_ref, True)], chunk)


def _rwkv_scan(r, v, kk, lw0, kd0, b0, lw1, kd1, b1):
    b, s, _ = r.shape
    t = min(SCAN_TILE, s)
    chunk = min(RWKV_CHUNK, t)
    n_tiles = s // t
    fwd = pl.BlockSpec((1, t, RWKV_W), lambda bi, i: (bi, i, 0))
    bwd = pl.BlockSpec((1, t, RWKV_W), lambda bi, i: (bi, n_tiles - 1 - i, 0))
    return pl.pallas_call(
        functools.partial(_rwkv_scan_kernel, chunk=chunk),
        name="rwkv_scan",
        out_shape=[jax.ShapeDtypeStruct((b, s, RWKV_W), F32)] * 2,
        grid=(b, n_tiles),
        in_specs=[fwd] * 6 + [bwd] * 6,
        out_specs=[fwd, bwd],
        scratch_shapes=[pltpu.VMEM((RWKV_W, RWKV_W), F32)] * 2,
        compiler_params=_cparams(("parallel", "arbitrary")),
    )(r, v, kk, lw0, kd0, b0, r, v, kk, lw1, kd1, b1)


def _mixout_ffn_kernel(x_ref, mod_ref, hf_ref, hr_ref, yb_ref, yf_ref, yr_ref, g_ref, bonus_ref, oa_ref,
                       wl_ref, wr_ref, wa_ref, ones_ref, lng_ref, lnb_ref, ng_ref, wi_ref, wo_ref, o_ref):
    o_lru = (hf_ref[...].astype(F32) + hr_ref[...].astype(F32)) * _gelu_tanh(yb_ref[...].astype(F32))
    y = yf_ref[...] + yr_ref[...]
    ones = ones_ref[...]
    mu = _dot_split(y, ones, 3) * (1.0 / RWKV_HD)
    dlt = y - mu
    var = _dot_split(dlt * dlt, ones, 2) * (1.0 / RWKV_HD)
    yn = dlt * lax.rsqrt(var + GN_EPS) * lng_ref[...] + lnb_ref[...]
    o_rwkv = (yn + bonus_ref[...].astype(F32)) * g_ref[...].astype(F32)
    acc = (_dot(o_lru, wl_ref[...]) + _dot(o_rwkv, wr_ref[...])
           + jnp.dot(oa_ref[...], wa_ref[...], preferred_element_type=F32))
    x_mid = x_ref[...] + mod_ref[0, 5:6, :] * acc
    o_ref[...] = _ffn_body(x_mid, mod_ref, ng_ref, wi_ref, wo_ref, 2, FFN_COLS)


def _mixout_ffn(x, mod, hf, hr, yb, yf, yr, g, bonus, oa, w_out, ln_g, ln_b, norm_g, w_ffn_in, w_ffn_out):
    b, s, _ = x.shape
    n = b * s
    tm = min(FFN_TOKENS, s)
    tok = lambda w: pl.BlockSpec((tm, w), lambda i: (i, 0))
    full = _resident
    ng = norm_g.reshape(1, D_MODEL)
    wl = w_out[:LRU_W]
    wr = w_out[LRU_W:LRU_W + RWKV_W]
    wa = w_out[LRU_W + RWKV_W:]
    ones = _block_ones(RWKV_W, RWKV_HD)
    lng = ln_g.reshape(1, RWKV_W)
    lnb = ln_b.reshape(1, RWKV_W)
    flat = lambda a: a.reshape(n, a.shape[-1])
    out = pl.pallas_call(
        _mixout_ffn_kernel,
        name="mix_out_ffn",
        out_shape=jax.ShapeDtypeStruct((n, D_MODEL), F32),
        grid=(n // tm,),
        in_specs=[
            tok(D_MODEL),
            pl.BlockSpec((1, N_MOD, D_MODEL), lambda i: ((i * tm) // s, 0, 0)),
            tok(LRU_W), tok(LRU_W), tok(LRU_W), tok(RWKV_W), tok(RWKV_W), tok(RWKV_W), tok(RWKV_W),
            tok(ATT_Q),
            full(wl), full(wr), full(wa), full(ones), full(lng), full(lnb),
            full(ng), full(w_ffn_in), full(w_ffn_out),
        ],
        out_specs=tok(D_MODEL),
        compiler_params=_cparams(("parallel",)),
    )(flat(x), mod, flat(hf), flat(hr), flat(yb), flat(yf), flat(yr), flat(g), flat(bonus), flat(oa),
      wl, wr, wa, ones, lng, lnb, ng, w_ffn_in, w_ffn_out)
    return out.reshape(b, s, D_MODEL)


def _layer(x, mod, lp, rope):
    x = _ffn(x, mod, 0, lp["norm_g"][0], lp["ffn_w_in"][0], lp["ffn_w_out"][0])
    rwkv = _rwkv_consts(lp["rwkv_mu"], lp["rwkv_w_up"], lp["rwkv_w0"], lp["rwkv_a_up"], lp["rwkv_a0"],
                        lp["rwkv_g_up"], lp["rwkv_k_k"], lp["rwkv_k_a"], lp["rwkv_r_k"])
    xb, yb, q, k, v, streams = _mixin(x, mod, lp["norm_g"][1], lp["w_mix_in"], lp["attn_q_norm"],
                                      lp["attn_k_norm"], rope, rwkv)
    r, vv, kk, g, bonus, lw0, kd0, b0, lw1, kd1, b1 = streams
    hf, hr = _lru(xb, lp["lru_conv_w"], lp["lru_conv_b"], *lp["lru_gates"])
    yf, yr = _rwkv_scan(r, vv, kk, lw0, kd0, b0, lw1, kd1, b1)
    oa = _attention(q, k, v)
    return _mixout_ffn(x, mod, hf, hr, yb, yf, yr, g, bonus, oa, lp["w_mix_out"], lp["rwkv_ln_g"],
                       lp["rwkv_ln_b"], lp["norm_g"][2], lp["ffn_w_in"][1], lp["ffn_w_out"][1])


def kernel(x_prompt, x_sample, c_prompt, c_sample, w_ada, b_ada, norm_g, ffn_w_in, ffn_w_out, w_mix_in, w_mix_out, lru_conv_w, lru_conv_b, lru_w_gate_a, lru_b_gate_a, lru_w_gate_x, lru_b_gate_x, lru_lambda, rwkv_mu, rwkv_w_up, rwkv_w0, rwkv_a_up, rwkv_a0, rwkv_g_up, rwkv_k_k, rwkv_k_a, rwkv_r_k, rwkv_ln_g, rwkv_ln_b, attn_q_norm, attn_k_norm):
    depth = w_ada.shape[0]
    bp, bs = x_prompt.shape[0], x_sample.shape[0]
    rows = -(-(bp + bs) // SUBLANES_V7X) * SUBLANES_V7X
    c_all = jnp.concatenate([c_prompt, c_sample, jnp.zeros((rows - bp - bs, D_MODEL), F32)], axis=0)
    mod_all = _ada_mod(c_all, w_ada, b_ada)

    params = {
        "norm_g": norm_g, "ffn_w_in": ffn_w_in.astype(BF16), "ffn_w_out": ffn_w_out.astype(BF16),
        "w_mix_in": w_mix_in.astype(BF16), "w_mix_out": w_mix_out.astype(BF16),
        "lru_conv_w": lru_conv_w, "lru_conv_b": lru_conv_b, "rwkv_mu": rwkv_mu, "rwkv_w_up": rwkv_w_up,
        "rwkv_w0": rwkv_w0, "rwkv_a_up": rwkv_a_up, "rwkv_a0": rwkv_a0, "rwkv_g_up": rwkv_g_up,
        "rwkv_k_k": rwkv_k_k, "rwkv_k_a": rwkv_k_a, "rwkv_r_k": rwkv_r_k, "rwkv_ln_g": rwkv_ln_g,
        "rwkv_ln_b": rwkv_ln_b, "attn_q_norm": attn_q_norm, "attn_k_norm": attn_k_norm,
    }
    outs = []
    for x, lo, nb in ((x_prompt, 0, bp), (x_sample, bp, bs)):
        rope = _rope_tables(x.shape[1])
        for l in range(depth):
            lp = {name: arr[l] for name, arr in params.items()}
            lp["lru_gates"] = _lru_gate_weights(lru_w_gate_a[l], lru_b_gate_a[l], lru_w_gate_x[l],
                                                lru_b_gate_x[l], lru_lambda[l])
            mod = mod_all[l, lo:lo + nb].reshape(nb, N_MOD, D_MODEL)
            x = _layer(x, mod, lp, rope)
        outs.append(x)
    return tuple(outs)
```
